```python
import math, functools
import jax, jax.numpy as jnp
from jax import lax
import numpy as np

D_MODEL = 1024
BATCH = 4
SEQ = 4096
DEPTH = 4
DEC_BATCH = 128
DEC_SEQ = 1
PAST_LEN = 2048
PAGE_SIZE = 128

N_META = 16
H_RWKV = 8
N_RWKV = 64
D_RWKV = H_RWKV * N_RWKV
LORA_W = 64
LORA_A = 64
RWKV_COLS = 3 * D_RWKV + LORA_W + LORA_A
DECAY_SCALE = 0.606531
LNX_EPS = 64e-5
H_ATT = 4
DH_ATT = 64
DV_ATT = 2 * DH_ATT
D_QK = 2 * H_ATT * DH_ATT
D_VA = H_ATT * DV_ATT
Q_BLOCK = 128
ATT_SCALE = DH_ATT ** -0.5
NEG = -1e30
D_FF = 4 * D_MODEL
Q0 = RWKV_COLS
K0 = Q0 + D_QK
V0 = K0 + D_QK
G0 = V0 + D_VA
D_IN = G0 + 2 * D_MODEL
RMS_EPS = 1e-6

kernel_name = "hybrid_rwkv7_diffattn_step"


def rms_norm(x, g):
    xf = x.astype(jnp.float32)
    y = xf * lax.rsqrt(jnp.mean(xf * xf, axis=-1, keepdims=True) + RMS_EPS) * g.astype(jnp.float32)
    return y.astype(x.dtype)


def wkv_scan(r, w, k, v, kk, a, s0):
    def step(s, inp):
        r_t, w_t, k_t, v_t, kk_t, a_t = inp
        sk = jnp.einsum('bhvk,bhk->bhv', s, kk_t)
        s = (s * w_t[:, :, None, :] - sk[..., None] * (kk_t * a_t)[:, :, None, :]
             + v_t[..., None] * k_t[:, :, None, :])
        return s, jnp.einsum('bhvk,bhk->bhv', s, r_t)
    xs = tuple(t.transpose(1, 0, 2, 3) for t in (r, w, k, v, kk, a))
    s_final, y = lax.scan(step, s0, xs)
    return y.transpose(1, 0, 2, 3), s_final


def rwkv_branch(p, prev, s0, W):
    f32 = jnp.float32
    B_, T = p.shape[0], p.shape[1]
    pf, qf = p.astype(f32), prev.astype(f32)
    m = pf + (qf - pf) * W['shift_mu'].astype(f32)
    r = m[..., :D_RWKV]
    k = m[..., D_RWKV:2 * D_RWKV]
    v = m[..., 2 * D_RWKV:3 * D_RWKV]
    w_lo = m[..., 3 * D_RWKV:3 * D_RWKV + LORA_W]
    a_lo = m[..., 3 * D_RWKV + LORA_W:]
    w = jnp.exp(-DECAY_SCALE * jax.nn.sigmoid(W['w0'].astype(f32) + jnp.tanh(w_lo) @ W['w_w2'].astype(f32)))
    a = jax.nn.sigmoid(W['a0'].astype(f32) + a_lo @ W['w_a2'].astype(f32))
    kk = k * W['k_k'].astype(f32)
    k = k * (1.0 + (a - 1.0) * W['k_a'].astype(f32))
    hs = lambda t: t.reshape(B_, T, H_RWKV, N_RWKV)
    r, w, k, v, kk, a = map(hs, (r, w, k, v, kk, a))
    kk = kk / jnp.maximum(jnp.sqrt(jnp.sum(kk * kk, axis=-1, keepdims=True)), 1e-12)
    y, s_final = wkv_scan(r, w, k, v, kk, a, s0.astype(f32))
    mu = jnp.mean(y, axis=-1, keepdims=True)
    var = jnp.mean(jnp.square(y - mu), axis=-1, keepdims=True)
    y = (y - mu) * lax.rsqrt(var + LNX_EPS)
    y = y.reshape(B_, T, D_RWKV) * W['lnx_g'].astype(f32) + W['lnx_b'].astype(f32)
    bonus = jnp.sum(r * k * W['r_k'].astype(f32), axis=-1, keepdims=True) * v
    return y + bonus.reshape(B_, T, D_RWKV), s_final


def diff_attend(q, k, v, mask, lam, lam_init, subln_g):
    f32 = jnp.float32
    s = jnp.einsum('bqhd,bkhd->bhqk', q.astype(f32), k.astype(f32)) * ATT_SCALE
    p = jax.nn.softmax(jnp.where(mask, s, NEG), axis=-1)
    B_, _, Q, K = p.shape
    p = p.reshape(B_, H_ATT, 2, Q, K)
    o = jnp.einsum('bhjqk,bkhe->bqhje', p, v.astype(f32))
    o = o[..., 0, :] - lam * o[..., 1, :]
    o = o * lax.rsqrt(jnp.mean(o * o, axis=-1, keepdims=True) + RMS_EPS) * subln_g.astype(f32)
    return (o * (1.0 - lam_init)).reshape(B_, Q, H_ATT * DV_ATT)


def attend_prompt(q, k, v, lam, lam_init, subln_g):
    B_, T = q.shape[0], q.shape[1]
    nb = -(-T // Q_BLOCK)
    tp = nb * Q_BLOCK
    pad = ((0, 0), (0, tp - T), (0, 0), (0, 0))
    qp, kp, vp = jnp.pad(q, pad), jnp.pad(k, pad), jnp.pad(v, pad)
    qb = qp.reshape(B_, nb, Q_BLOCK, 2 * H_ATT, DH_ATT).transpose(1, 0, 2, 3, 4)
    kpos = jnp.arange(tp)

    def one(args):
        qblk, start = args
        qpos = start + jnp.arange(Q_BLOCK)
        mask = kpos[None, :] <= qpos[:, None]
        return diff_attend(qblk, kp, vp, mask, lam, lam_init, subln_g)

    o = lax.map(one, (qb, jnp.arange(nb, dtype=jnp.int32) * Q_BLOCK))
    return o.transpose(1, 0, 2, 3).reshape(B_, tp, D_VA)[:, :T]


def attend_paged(q, k, v, lam, lam_init, subln_g, k_past, v_past):
    S, P = q.shape[1], k_past.shape[1]
    kk = jnp.concatenate([k_past.astype(k.dtype), k], axis=1)
    vv = jnp.concatenate([v_past.astype(v.dtype), v], axis=1)
    mask = jnp.concatenate([jnp.ones((S, P), bool), jnp.tril(jnp.ones((S, S), bool))], axis=1)
    return diff_attend(q, kk, vv, mask, lam, lam_init, subln_g)


def trunk_layer(x, W, lam, lam_init, shift_prev, wkv0, attend):
    B_, T = x.shape[0], x.shape[1]
    hn = rms_norm(x, W['norm_mix'])
    z = hn @ W['w_in']
    p_r = z[..., :RWKV_COLS]
    q = z[..., Q0:K0].reshape(B_, T, 2 * H_ATT, DH_ATT)
    k = z[..., K0:V0].reshape(B_, T, 2 * H_ATT, DH_ATT)
    v = z[..., V0:G0].reshape(B_, T, H_ATT, DV_ATT)
    gates = jax.nn.sigmoid(z[..., G0:].astype(jnp.float32))
    prev = jnp.concatenate([shift_prev[:, None].astype(p_r.dtype), p_r[:, :-1]], axis=1)
    o_a, wkv_new = rwkv_branch(p_r, prev, wkv0, W)
    o_b = attend(q, k, v, lam, lam_init, W['subln_g'])
    merged = (gates[..., :D_MODEL] * (o_a @ W['w_pa'].astype(jnp.float32))
              + gates[..., D_MODEL:] * (o_b @ W['w_pb'].astype(jnp.float32)))
    x = x + (merged.astype(x.dtype) @ W['w_o'])
    h2 = rms_norm(x, W['norm_mlp'])
    x = x + jnp.square(jax.nn.relu(h2 @ W['w_up'])) @ W['w_down']
    return x, k, v, wkv_new, p_r[:, -1]


def setup_inputs(seed: int = 0) -> dict:
    key = jax.random.key(seed)
    ks = iter(jax.random.split(key, 40))
    f32 = jnp.float32
    nrm = lambda shape, s: jax.random.normal(next(ks), shape, f32) * s
    n_pages = PAST_LEN // PAGE_SIZE
    n_used = DEC_BATCH * n_pages
    n_pool = (n_used * 5) // 4
    page_table = jax.random.permutation(next(ks), n_pool)[:n_used].reshape(DEC_BATCH, n_pages).astype(jnp.int32)
    L = DEPTH
    return {
        'x_prompt': nrm((BATCH, SEQ, D_MODEL), 1.0),
        'x_sample': nrm((DEC_BATCH, DEC_SEQ, D_MODEL), 1.0),
        'cache_k': nrm((L, n_pool, PAGE_SIZE, 2 * H_ATT, DH_ATT), 1.0),
        'cache_v': nrm((L, n_pool, PAGE_SIZE, H_ATT, DV_ATT), 1.0),
        'state_wkv': nrm((L, DEC_BATCH, H_RWKV, N_RWKV, N_RWKV), 0.5),
        'state_shift': nrm((L, DEC_BATCH, RWKV_COLS), 1.0),
        'page_table': page_table,
        'meta_tokens': nrm((N_META, D_MODEL), 1.0),
        'norm_mix': 1.0 + nrm((L, D_MODEL), 0.02),
        'w_in': nrm((L, D_MODEL, D_IN), D_MODEL ** -0.5),
        'shift_mu': jax.random.uniform(next(ks), (L, RWKV_COLS), f32),
        'w0': nrm((L, D_RWKV), 1.0),
        'w_w2': nrm((L, LORA_W, D_RWKV), 0.1 * LORA_W ** -0.5),
        'a0': nrm((L, D_RWKV), 0.5),
        'w_a2': nrm((L, LORA_A, D_RWKV), 0.1 * LORA_A ** -0.5),
        'k_k': 0.85 + nrm((L, D_RWKV), 0.05),
        'k_a': 1.0 + nrm((L, D_RWKV), 0.05),
        'r_k': nrm((L, H_RWKV, N_RWKV), 0.1),
        'lnx_g': 1.0 + nrm((L, D_RWKV), 0.02),
        'lnx_b': nrm((L, D_RWKV), 0.02),
        'lambda_q1': nrm((L, DH_ATT), 0.1),
        'lambda_k1': nrm((L, DH_ATT), 0.1),
        'lambda_q2': nrm((L, DH_ATT), 0.1),
        'lambda_k2': nrm((L, DH_ATT), 0.1),
        'subln_g': 1.0 + nrm((L, DV_ATT), 0.02),
        'w_pa': nrm((L, D_RWKV, D_MODEL), D_RWKV ** -0.5),
        'w_pb': nrm((L, D_VA, D_MODEL), D_VA ** -0.5),
        'w_o': nrm((L, D_MODEL, D_MODEL), D_MODEL ** -0.5),
        'norm_mlp': 1.0 + nrm((L, D_MODEL), 0.02),
        'w_up': nrm((L, D_MODEL, D_FF), D_MODEL ** -0.5),
        'w_down': nrm((L, D_FF, D_MODEL), D_FF ** -0.5),
        'norm_final': 1.0 + nrm((D_MODEL,), 0.02),
    }


def reference(x_prompt, x_sample, cache_k, cache_v, state_wkv, state_shift, page_table,
              meta_tokens, norm_mix, w_in, shift_mu, w0, w_w2, a0, w_a2, k_k, k_a, r_k,
              lnx_g, lnx_b, lambda_q1, lambda_k1, lambda_q2, lambda_k2, subln_g,
              w_pa, w_pb, w_o, norm_mlp, w_up, w_down, norm_final):
    f32 = jnp.float32
    B_, DB = x_prompt.shape[0], x_sample.shape[0]
    n_pages = page_table.shape[1]
    xp = jnp.concatenate([jnp.broadcast_to(meta_tokens[None].astype(x_prompt.dtype), (B_, N_META, D_MODEL)),
                          x_prompt], axis=1)
    xs = x_sample
    kp_l, vp_l, wp_l, sp_l, ks_l, vs_l, ws_l, ss_l = ([] for _ in range(8))
    for l in range(DEPTH):
        W = {'norm_mix': norm_mix[l], 'w_in': w_in[l], 'shift_mu': shift_mu[l], 'w0': w0[l],
             'w_w2': w_w2[l], 'a0': a0[l], 'w_a2': w_a2[l], 'k_k': k_k[l], 'k_a': k_a[l],
             'r_k': r_k[l], 'lnx_g': lnx_g[l], 'lnx_b': lnx_b[l], 'subln_g': subln_g[l],
             'w_pa': w_pa[l], 'w_pb': w_pb[l], 'w_o': w_o[l], 'norm_mlp': norm_mlp[l],
             'w_up': w_up[l], 'w_down': w_down[l]}
        lam_init = 0.8 - 0.6 * math.exp(-0.3 * l)
        lam = (jnp.exp(jnp.sum(lambda_q1[l].astype(f32) * lambda_k1[l].astype(f32)))
               - jnp.exp(jnp.sum(lambda_q2[l].astype(f32) * lambda_k2[l].astype(f32))) + lam_init)
        xp, k_new, v_new, wkv_new, sh_new = trunk_layer(
            xp, W, lam, lam_init, jnp.zeros((B_, RWKV_COLS), xp.dtype),
            jnp.zeros((B_, H_RWKV, N_RWKV, N_RWKV), f32), attend_prompt)
        kp_l.append(k_new); vp_l.append(v_new); wp_l.append(wkv_new); sp_l.append(sh_new)
        k_past = cache_k[l][page_table].reshape(DB, n_pages * PAGE_SIZE, 2 * H_ATT, DH_ATT)
        v_past = cache_v[l][page_table].reshape(DB, n_pages * PAGE_SIZE, H_ATT, DV_ATT)
        xs, k_new, v_new, wkv_new, sh_new = trunk_layer(
            xs, W, lam, lam_init, state_shift[l], state_wkv[l],
            functools.partial(attend_paged, k_past=k_past, v_past=v_past))
        ks_l.append(k_new); vs_l.append(v_new)
        ws_l.append(wkv_new.astype(state_wkv.dtype)); ss_l.append(sh_new)
    y_prompt = rms_norm(xp, norm_final)[:, N_META:]
    y_sample = rms_norm(xs, norm_final)
    return (y_prompt, y_sample,
            jnp.stack(kp_l), jnp.stack(vp_l), jnp.stack(wp_l), jnp.stack(sp_l),
            jnp.stack(ks_l), jnp.stack(vs_l), jnp.stack(ws_l), jnp.stack(ss_l))
```

```python
import functools
import math

import jax
import jax.numpy as jnp
from jax import lax
from jax.experimental import pallas as pl
from jax.experimental.pallas import tpu as pltpu

F32 = jnp.float32
BF16 = jnp.bfloat16
HI = lax.Precision.HIGHEST

H_RWKV = 8
N_RWKV = 64
D_RWKV = H_RWKV * N_RWKV
LORA = 64
RWKV_COLS = 3 * D_RWKV + 2 * LORA
DECAY_SCALE = 0.606531
LNX_EPS = 64e-5
H_ATT = 4
DH_ATT = 64
DV_ATT = 2 * DH_ATT
D_QK = 2 * H_ATT * DH_ATT
D_VA = H_ATT * DV_ATT
ATT_SCALE = DH_ATT ** -0.5
NEG = -1e30
RMS_EPS = 1e-6

LANES = 128
VMEM_LIMIT = 56 * 1024 * 1024


def _cparams(sem):
    return pltpu.CompilerParams(dimension_semantics=sem, vmem_limit_bytes=VMEM_LIMIT)


def _pick(n, cands):
    for c in cands:
        if n % c == 0:
            return c
    return n


def _const_spec(shape):
    nd = len(shape)
    return pl.BlockSpec(shape, lambda *_: (0,) * nd, pipeline_mode=pl.Buffered(1))


def _seg_ones(n, seg, scale=1.0):
    i = lax.broadcasted_iota(jnp.int32, (n, n), 0) // seg
    j = lax.broadcasted_iota(jnp.int32, (n, n), 1) // seg
    return jnp.where(i == j, scale, 0.0).astype(F32)


def _inproj_kernel(x_ref, g_ref, w_ref, pr_ref, q_ref, k_ref, v_ref, kb_ref, vb_ref, gate_ref, *, d_model):
    x = x_ref[...]
    hn = x * lax.rsqrt(jnp.mean(x * x, axis=-1, keepdims=True) + RMS_EPS) * g_ref[...]
    hb = hn.astype(BF16)

    def proj(lo, n):
        return jnp.dot(hb, w_ref[:, lo:lo + n], preferred_element_type=F32)

    q0 = RWKV_COLS
    k0 = q0 + D_QK
    v0 = k0 + D_QK
    g0 = v0 + D_VA
    pr_ref[...] = proj(0, RWKV_COLS)
    q_ref[...] = (proj(q0, D_QK) * ATT_SCALE).astype(q_ref.dtype)
    k = proj(k0, D_QK)
    k_ref[...] = k
    kb_ref[...] = k.astype(BF16)
    v = proj(v0, D_VA)
    v_ref[...] = v
    vb_ref[...] = v.astype(BF16)
    gate_ref[...] = jax.nn.sigmoid(proj(g0, 2 * d_model))


def _inproj(x2, g, w_bf, q_dtype):
    m, d = x2.shape
    d_in = w_bf.shape[1]
    tm = _pick(m, (256, 128))
    row = lambda n: pl.BlockSpec((tm, n), lambda i: (i, 0))
    outs = [(RWKV_COLS, F32), (D_QK, q_dtype), (D_QK, F32), (D_VA, F32), (D_QK, BF16), (D_VA, BF16), (2 * d, F32)]
    return pl.pallas_call(
        functools.partial(_inproj_kernel, d_model=d),
        grid=(m // tm,),
        in_specs=[row(d), _const_spec((1, d)), _const_spec((d, d_in))],
        out_specs=[row(n) for n, _ in outs],
        out_shape=[jax.ShapeDtypeStruct((m, n), dt) for n, dt in outs],
        compiler_params=_cparams(("parallel",)),
        name="inproj",
    )(x2, g.reshape(1, d), w_bf)


def _prep_kernel(p_ref, prev_ref, mu_ref, w0_ref, ww2_ref, a0_ref, wa2_ref, kk_ref, ka_ref, rk_ref, sel_ref,
                 r_out, w_out, k_out, v_out, kk_out, kka_out, bonus_out, *, shift_rows):
    p = p_ref[0]
    if shift_rows:
        first = pl.program_id(1) == 0
        before = jnp.where(first, 0.0, prev_ref[0][7:8])
        rolled = pltpu.roll(p, 1, axis=0)
        rows = lax.broadcasted_iota(jnp.int32, p.shape, 0)
        prev = jnp.where(rows == 0, before, rolled)
    else:
        prev = prev_ref[0]
    m = p + (prev - p) * mu_ref[...]
    r = m[:, :D_RWKV]
    k = m[:, D_RWKV:2 * D_RWKV]
    v = m[:, 2 * D_RWKV:3 * D_RWKV]
    w_lo = m[:, 3 * D_RWKV:3 * D_RWKV + LORA]
    a_lo = m[:, 3 * D_RWKV + LORA:]
    sel = sel_ref[...]
    w = jnp.exp(-DECAY_SCALE * jax.nn.sigmoid(
        w0_ref[...] + jnp.dot(jnp.tanh(w_lo), ww2_ref[...], precision=HI, preferred_element_type=F32)))
    a = jax.nn.sigmoid(a0_ref[...] + jnp.dot(a_lo, wa2_ref[...], precision=HI, preferred_element_type=F32))
    kk = k * kk_ref[...]
    k2 = k * (1.0 + (a - 1.0) * ka_ref[...])
    ss = jnp.dot(kk * kk, sel, precision=HI, preferred_element_type=F32)
    kkn = kk / jnp.maximum(jnp.sqrt(ss), 1e-12)
    r_out[0] = r
    w_out[0] = w
    k_out[0] = k2
    v_out[0] = v
    kk_out[0] = kkn
    kka_out[0] = kkn * a
    bonus_out[0] = jnp.dot(r * k2 * rk_ref[...], sel, precision=HI, preferred_element_type=F32) * v


def _prep(p3, prev3, lw, sel512):
    b, tp, c = p3.shape
    shift_rows = prev3 is None
    tt = _pick(tp, (384, 256, 128))
    cur = pl.BlockSpec((1, tt, c), lambda i, j: (i, j, 0))
    if shift_rows:
        nb8 = tt // 8
        prev_spec = pl.BlockSpec((1, 8, c), lambda i, j: (i, jnp.maximum(j * nb8 - 1, 0), 0))
        prev_arr = p3
    else:
        prev_spec = cur
        prev_arr = prev3
    out = pl.BlockSpec((1, tt, D_RWKV), lambda i, j: (i, j, 0))
    vec = lambda n: _const_spec((1, n))
    return pl.pallas_call(
        functools.partial(_prep_kernel, shift_rows=shift_rows),
        grid=(b, tp // tt),
        in_specs=[cur, prev_spec, vec(c), vec(D_RWKV), _const_spec((LORA, D_RWKV)), vec(D_RWKV),
                  _const_spec((LORA, D_RWKV)), vec(D_RWKV), vec(D_RWKV), vec(D_RWKV),
                  _const_spec((D_RWKV, D_RWKV))],
        out_specs=[out] * 7,
        out_shape=[jax.ShapeDtypeStruct((b, tp, D_RWKV), F32)] * 7,
        compiler_params=_cparams(("parallel", "parallel")),
        name="prep",
    )(p3, prev_arr, lw['shift_mu'], lw['w0'], lw['w_w2'], lw['a0'], lw['w_a2'], lw['k_k'], lw['k_a'],
      lw['r_k'], sel512)


WKV_W = 4 * N_RWKV


def _wkv_kernel(r_ref, w_ref, k_ref, v_ref, kk_ref, kka_ref, s0_ref, y_ref, sT_ref, s_scr, ybuf, *, t_valid, nb, tc,
                gsz):
    c = pl.program_id(1)
    nq = D_RWKV // WKV_W

    @pl.when(c == 0)
    def _():
        s_scr[...] = s0_ref[...].reshape(s_scr.shape)

    nsteps = jnp.clip(t_valid - c * tc, 0, tc)

    @pl.when(nsteps < tc)
    def _():
        y_ref[...] = jnp.zeros_like(y_ref)

    sel = _seg_ones(WKV_W, N_RWKV).astype(BF16)
    eye = jnp.where(lax.broadcasted_iota(jnp.int32, (N_RWKV, WKV_W), 0)
                    == lax.broadcasted_iota(jnp.int32, (N_RWKV, WKV_W), 1) % N_RWKV, 1.0, 0.0).astype(F32)[None]
    chains = [(b, q) for b in range(nb) for q in range(nq)]

    def seg_sum(x3, pieces):
        x = x3.reshape(-1, WKV_W)
        hi = x.astype(BF16)
        if pieces == 1:
            o = jnp.dot(hi, sel, preferred_element_type=F32)
        else:
            lo = (x - hi.astype(F32)).astype(BF16)
            o = jnp.dot(jnp.concatenate([hi, lo], axis=0), sel, preferred_element_type=F32)
            o = o[:x.shape[0]] + o[x.shape[0]:]
        return o.reshape(x3.shape)

    def group(g, carry):
        t0 = pl.multiple_of(g * gsz, gsz)

        def rows(ref, i):
            return jnp.stack([ref[b, pl.ds(t0, gsz), WKV_W * q:WKV_W * (q + 1)][i:i + 1] for b, q in chains])

        for i in range(gsz):
            vb = seg_sum(eye * rows(v_ref, i), 2)
            s = s_scr[...]
            sk = seg_sum(s * rows(kk_ref, i), 2)
            s = s * rows(w_ref, i) - sk * rows(kka_ref, i) + vb * rows(k_ref, i)
            s_scr[...] = s
            yb = seg_sum(s * rows(r_ref, i), 1)
            ybuf[:, i:i + 1, :] = jnp.sum(yb * eye, axis=1, keepdims=True)
        for n, (b, q) in enumerate(chains):
            y_ref[b, pl.ds(t0, gsz), WKV_W * q:WKV_W * (q + 1)] = ybuf[n]
        return carry

    lax.fori_loop(0, nsteps // gsz, group, 0)

    @pl.when(c == pl.num_programs(1) - 1)
    def _():
        sT_ref[...] = s_scr[...].reshape(sT_ref.shape)


def _wkv(r, w, k, v, kk, kka, s0, t_valid):
    b, tp, _ = r.shape
    nq = D_RWKV // WKV_W
    nb = _pick(b, (4, 2, 1))
    tc = _pick(tp, (128,))
    gsz = 8 if tc % 8 == 0 else tc
    assert t_valid % gsz == 0, "sequence length must fill whole 8-step groups"
    seq = pl.BlockSpec((nb, tc, D_RWKV), lambda i, c: (i, c, 0))
    st = pl.BlockSpec((nb, nq, N_RWKV, WKV_W), lambda i, c: (i, 0, 0, 0))
    return pl.pallas_call(
        functools.partial(_wkv_kernel, t_valid=t_valid, nb=nb, tc=tc, gsz=gsz),
        grid=(b // nb, tp // tc),
        in_specs=[seq] * 6 + [st],
        out_specs=[seq, st],
        out_shape=[jax.ShapeDtypeStruct((b, tp, D_RWKV), F32),
                   jax.ShapeDtypeStruct((b, nq, N_RWKV, WKV_W), F32)],
        scratch_shapes=[pltpu.VMEM((nb * nq, N_RWKV, WKV_W), F32),
                        pltpu.VMEM((nb * nq, gsz, WKV_W), F32)],
        compiler_params=_cparams(("parallel", "arbitrary")),
        name="wkv",
    )(r, w, k, v, kk, kka, s0)


def _to_groups(s):
    b = s.shape[0]
    nq = D_RWKV // WKV_W
    return s.reshape(b, nq, H_RWKV // nq, N_RWKV, N_RWKV).transpose(0, 1, 3, 2, 4).reshape(b, nq, N_RWKV, WKV_W)


def _from_groups(s):
    b = s.shape[0]
    nq = D_RWKV // WKV_W
    return s.reshape(b, nq, N_RWKV, H_RWKV // nq, N_RWKV).transpose(0, 1, 3, 2, 4).reshape(
        b, H_RWKV, N_RWKV, N_RWKV)


def _lam(lq1, lk1, lq2, lk2, lam_init):
    return (jnp.exp(jnp.sum(lq1 * lk1, axis=-1, keepdims=True))
            - jnp.exp(jnp.sum(lq2 * lk2, axis=-1, keepdims=True)) + lam_init)


def _subln(o, g, lam_init):
    return o * lax.rsqrt(jnp.mean(o * o, axis=-1, keepdims=True) + RMS_EPS) * g * (1.0 - lam_init)


def _flash_kernel(q_ref, k_ref, v_ref, lq1, lk1, lq2, lk2, g_ref, o_ref, m_scr, l_scr, acc_scr, *, lam_init, tq):
    qi = pl.program_id(1)
    ki = pl.program_id(2)

    @pl.when(ki == 0)
    def _():
        m_scr[...] = jnp.full_like(m_scr, NEG)
        l_scr[...] = jnp.zeros_like(l_scr)
        acc_scr[...] = jnp.zeros_like(acc_scr)

    def block(masked):
        lane = lax.broadcasted_iota(jnp.int32, (tq, LANES), 1)
        if masked:
            keep = (lax.broadcasted_iota(jnp.int32, (tq, tq), 1) <= lax.broadcasted_iota(jnp.int32, (tq, tq), 0))
        for h in range(H_ATT):
            ln = slice(LANES * h, LANES * (h + 1))
            qh = q_ref[0, :, ln]
            kh = k_ref[0, :, ln]
            vh = v_ref[0, :, ln]
            for sub in range(2):
                j = 2 * h + sub
                ks = jnp.where((lane < DH_ATT) if sub == 0 else (lane >= DH_ATT), kh, jnp.zeros_like(kh))
                s = lax.dot_general(qh, ks, (((1,), (1,)), ((), ())), preferred_element_type=F32)
                if masked:
                    s = jnp.where(keep, s, NEG)
                m_prev = m_scr[j]
                m_new = jnp.maximum(m_prev, jnp.max(s, axis=-1, keepdims=True))
                alpha = jnp.exp(m_prev - m_new)
                p = jnp.exp(s - m_new[:, :1])
                l_scr[j] = alpha * l_scr[j] + jnp.sum(p, axis=-1, keepdims=True)
                acc_scr[j] = alpha * acc_scr[j] + jnp.dot(p.astype(BF16), vh, preferred_element_type=F32)
                m_scr[j] = m_new

    @pl.when(ki < qi)
    def _():
        block(False)

    @pl.when(ki == qi)
    def _():
        block(True)
        lam = _lam(lq1[...], lk1[...], lq2[...], lk2[...], lam_init)
        for h in range(H_ATT):
            o1 = acc_scr[2 * h] / l_scr[2 * h]
            o2 = acc_scr[2 * h + 1] / l_scr[2 * h + 1]
            o_ref[0, :, LANES * h:LANES * (h + 1)] = _subln(o1 - lam * o2, g_ref[...], lam_init)


def _flash(qb, kb, vb, lam_rows, g, lam_init):
    b, tp, _ = qb.shape
    tq = _pick(tp, (384, 256, 128))
    nq = tp // tq
    qspec = pl.BlockSpec((1, tq, D_QK), lambda i, q, k: (i, q, 0))
    kspec = pl.BlockSpec((1, tq, D_QK), lambda i, q, k: (i, jnp.minimum(k, q), 0))
    vec = _const_spec((1, DH_ATT))
    return pl.pallas_call(
        functools.partial(_flash_kernel, lam_init=lam_init, tq=tq),
        grid=(b, nq, nq),
        in_specs=[qspec, kspec, kspec, vec, vec, vec, vec, _const_spec((1, DV_ATT))],
        out_specs=qspec,
        out_shape=jax.ShapeDtypeStruct((b, tp, D_VA), F32),
        scratch_shapes=[pltpu.VMEM((2 * H_ATT, tq, LANES), F32)] * 3,
        compiler_params=_cparams(("parallel", "parallel", "arbitrary")),
        name="flash",
    )(qb, kb, vb, *lam_rows, g)


def _paged_kernel(pt_ref, q_ref, kn_ref, vn_ref, kc_ref, vc_ref, lq1, lk1, lq2, lk2, g_ref, o_ref,
                  q8_scr, m_scr, l_scr, acc_scr, *, lam_init):
    p = pl.program_id(1)
    nsub = q8_scr.shape[0]

    @pl.when(p == 0)
    def _():
        rows = lax.broadcasted_iota(jnp.int32, (nsub, D_QK), 0)
        cols = lax.broadcasted_iota(jnp.int32, (nsub, D_QK), 1) // DH_ATT
        q8 = jnp.where(rows == cols, q_ref[0], 0.0)
        q8_scr[...] = q8
        m_scr[...] = jnp.sum(q8 * kn_ref[0], axis=-1, keepdims=True)
        l_scr[...] = jnp.ones_like(l_scr)
        acc_scr[...] = jnp.broadcast_to(vn_ref[0], acc_scr.shape)

    s = lax.dot_general(q8_scr[...], kc_ref[...], (((1,), (1,)), ((), ())), precision=HI,
                        preferred_element_type=F32)
    m_prev = m_scr[...]
    m_new = jnp.maximum(m_prev, jnp.max(s, axis=-1, keepdims=True))
    alpha = jnp.exp(m_prev - m_new)
    pr = jnp.exp(s - m_new)
    l_scr[...] = alpha * l_scr[...] + jnp.sum(pr, axis=-1, keepdims=True)
    acc_scr[...] = alpha * acc_scr[...] + jnp.dot(pr.astype(BF16), vc_ref[...].astype(BF16),
                                                 preferred_element_type=F32)
    m_scr[...] = m_new

    @pl.when(p == pl.num_programs(1) - 1)
    def _():
        lam = _lam(lq1[...], lk1[...], lq2[...], lk2[...], lam_init)
        o = acc_scr[...] / l_scr[...]
        for h in range(H_ATT):
            ln = slice(LANES * h, LANES * (h + 1))
            d = o[2 * h:2 * h + 1, ln] - lam * o[2 * h + 1:2 * h + 2, ln]
            o_ref[0, :, ln] = _subln(d, g_ref[...], lam_init)


def _paged(q, kn, vn, cache_k4, cache_v4, layer, page_table, lam_rows, g, lam_init):
    db = q.shape[0]
    n_pages = page_table.shape[1]
    page = cache_k4.shape[2]
    tok = pl.BlockSpec((1, 1, D_QK), lambda b, p, pt: (b, 0, 0))
    pg = pl.BlockSpec((None, None, page, D_QK), lambda b, p, pt: (layer, pt[b * n_pages + p], 0, 0))
    vec = pl.BlockSpec((1, DH_ATT), lambda b, p, pt: (0, 0))
    nsub = 16
    grid_spec = pltpu.PrefetchScalarGridSpec(
        num_scalar_prefetch=1,
        grid=(db, n_pages),
        in_specs=[tok, tok, tok, pg, pg, vec, vec, vec, vec, pl.BlockSpec((1, DV_ATT), lambda b, p, pt: (0, 0))],
        out_specs=tok,
        scratch_shapes=[pltpu.VMEM((nsub, D_QK), F32), pltpu.VMEM((nsub, 1), F32), pltpu.VMEM((nsub, 1), F32),
                        pltpu.VMEM((nsub, D_VA), F32)],
    )
    return pl.pallas_call(
        functools.partial(_paged_kernel, lam_init=lam_init),
        grid_spec=grid_spec,
        out_shape=jax.ShapeDtypeStruct((db, 1, D_VA), F32),
        compiler_params=_cparams(("parallel", "arbitrary")),
        name="paged",
    )(page_table.reshape(-1), q, kn, vn, cache_k4, cache_v4, *lam_rows, g)


def _mix_kernel(x_ref, y_ref, bonus_ref, ob_ref, gate_ref, lng_ref, lnb_ref, sel_ref, wpa_ref, wpb_ref, wo_ref,
                nm_ref, wup_ref, wdn_ref, nf_ref, xo_ref, *, d_model, ff_chunk, final):
    sel = sel_ref[...] * (1.0 / N_RWKV)
    y = y_ref[...]
    mu = jnp.dot(y, sel, precision=HI, preferred_element_type=F32)
    d = y - mu
    var = jnp.dot(d * d, sel, precision=HI, preferred_element_type=F32)
    o_a = d * lax.rsqrt(var + LNX_EPS) * lng_ref[...] + lnb_ref[...] + bonus_ref[...]
    gates = gate_ref[...]
    merged = (gates[:, :d_model] * jnp.dot(o_a.astype(BF16), wpa_ref[...], preferred_element_type=F32)
              + gates[:, d_model:] * jnp.dot(ob_ref[...].astype(BF16), wpb_ref[...], preferred_element_type=F32))
    x = x_ref[...] + jnp.dot(merged.astype(BF16), wo_ref[...], preferred_element_type=F32)
    h2 = (x * lax.rsqrt(jnp.mean(x * x, axis=-1, keepdims=True) + RMS_EPS) * nm_ref[...]).astype(BF16)
    d_ff = wup_ref.shape[1]
    for c in range(d_ff // ff_chunk):
        cs = slice(c * ff_chunk, (c + 1) * ff_chunk)
        u = jnp.maximum(jnp.dot(h2, wup_ref[:, cs], preferred_element_type=F32), 0.0)
        x = x + jnp.dot((u * u).astype(BF16), wdn_ref[cs, :], preferred_element_type=F32)
    if final:
        x = x * lax.rsqrt(jnp.mean(x * x, axis=-1, keepdims=True) + RMS_EPS) * nf_ref[...]
    xo_ref[...] = x


def _mix(x2, y2, bonus2, ob2, gates2, lw, sel512, norm_final, final):
    m, d = x2.shape
    d_ff = lw['w_up_b'].shape[1]
    tm = _pick(m, (256, 128))
    row = lambda n: pl.BlockSpec((tm, n), lambda i: (i, 0))
    cs = _const_spec
    return pl.pallas_call(
        functools.partial(_mix_kernel, d_model=d, ff_chunk=min(d_ff, 1024), final=final),
        grid=(m // tm,),
        in_specs=[row(d), row(D_RWKV), row(D_RWKV), row(D_VA), row(2 * d), cs((1, D_RWKV)), cs((1, D_RWKV)),
                  cs((D_RWKV, D_RWKV)), cs((D_RWKV, d)), cs((D_VA, d)), cs((d, d)), cs((1, d)),
                  cs((d, d_ff)), cs((d_ff, d)), cs((1, d))],
        out_specs=row(d),
        out_shape=jax.ShapeDtypeStruct((m, d), F32),
        compiler_params=_cparams(("parallel",)),
        name="mix",
    )(x2, y2, bonus2, ob2, gates2, lw['lnx_g'], lw['lnx_b'], sel512, lw['w_pa_b'], lw['w_pb_b'], lw['w_o_b'],
      lw['norm_mlp'], lw['w_up_b'], lw['w_down_b'], norm_final)


def _layer(x3, lw, sel512, lam_init, t_valid, prev3, s0_pairs, attend, norm_final, final):
    b, tp, d = x3.shape
    m = b * tp
    pr, q, k, v, kb, vb, gates = _inproj(x3.reshape(m, d), lw['norm_mix'], lw['w_in_b'],
                                         BF16 if prev3 is None else F32)
    three = lambda a: a.reshape(b, tp, a.shape[-1])
    pr3 = three(pr)
    r, w, k2, vv, kk, kka, bonus = _prep(pr3, prev3, lw, sel512)
    if prev3 is not None:
        seqs = lambda a: a.reshape(m, 1, D_RWKV)
        y, s_fin = _wkv(seqs(r), seqs(w), seqs(k2), seqs(vv), seqs(kk), seqs(kka), s0_pairs, 1)
    else:
        y, s_fin = _wkv(r, w, k2, vv, kk, kka, s0_pairs, t_valid)
    ob = attend(three(q), three(k), three(v), three(kb), three(vb))
    xo = _mix(x3.reshape(m, d), y.reshape(m, D_RWKV), bonus.reshape(m, D_RWKV), ob.reshape(m, D_VA), gates,
              lw, sel512, norm_final, final)
    return xo.reshape(b, tp, d), three(k), three(v), s_fin, pr3


def kernel(x_prompt, x_sample, cache_k, cache_v, state_wkv, state_shift, page_table, meta_tokens, norm_mix, w_in,
           shift_mu, w0, w_w2, a0, w_a2, k_k, k_a, r_k, lnx_g, lnx_b, lambda_q1, lambda_k1, lambda_q2, lambda_k2,
           subln_g, w_pa, w_pb, w_o, norm_mlp, w_up, w_down, norm_final):
    bsz, seq, d = x_prompt.shape
    db = x_sample.shape[0]
    depth = w_in.shape[0]
    n_meta = meta_tokens.shape[0]
    t = seq + n_meta
    tp = -(-t // LANES) * LANES
    n_pool, page = cache_k.shape[1], cache_k.shape[2]

    xp = jnp.concatenate([jnp.broadcast_to(meta_tokens[None].astype(x_prompt.dtype), (bsz, n_meta, d)), x_prompt,
                          jnp.zeros((bsz, tp - t, d), x_prompt.dtype)], axis=1)
    xs = x_sample.reshape(1, db, d)
    cache_k4 = cache_k.reshape(depth, n_pool, page, D_QK)
    cache_v4 = cache_v.reshape(depth, n_pool, page, D_VA)
    sel512 = _seg_ones(D_RWKV, N_RWKV)
    nf = norm_final.reshape(1, d)
    zero_state = jnp.zeros((bsz, D_RWKV // WKV_W, N_RWKV, WKV_W), F32)

    kp_l, vp_l, wp_l, sp_l, ks_l, vs_l, ws_l, ss_l = ([] for _ in range(8))
    for l in range(depth):
        row = lambda a: a[l].reshape(1, -1)
        lw = {
            'norm_mix': norm_mix[l], 'w_in_b': w_in[l].astype(BF16), 'shift_mu': row(shift_mu), 'w0': row(w0),
            'w_w2': w_w2[l], 'a0': row(a0), 'w_a2': w_a2[l], 'k_k': row(k_k), 'k_a': row(k_a), 'r_k': row(r_k),
            'lnx_g': row(lnx_g), 'lnx_b': row(lnx_b), 'w_pa_b': w_pa[l].astype(BF16),
            'w_pb_b': w_pb[l].astype(BF16), 'w_o_b': w_o[l].astype(BF16), 'norm_mlp': row(norm_mlp),
            'w_up_b': w_up[l].astype(BF16), 'w_down_b': w_down[l].astype(BF16),
        }
        lam_init = 0.8 - 0.6 * math.exp(-0.3 * l)
        lam_rows = (row(lambda_q1), row(lambda_k1), row(lambda_q2), row(lambda_k2))
        g = row(subln_g)
        final = l == depth - 1

        attend_p = lambda q, k, v, kb, vb: _flash(q, kb, vb, lam_rows, g, lam_init)
        xp, k_new, v_new, s_fin, pr3 = _layer(xp, lw, sel512, lam_init, t, None, zero_state, attend_p, nf, final)
        kp_l.append(k_new[:, :t].reshape(bsz, t, 2 * H_ATT, DH_ATT))
        vp_l.append(v_new[:, :t].reshape(bsz, t, H_ATT, DV_ATT))
        wp_l.append(_from_groups(s_fin))
        sp_l.append(pr3[:, t - 1])

        attend_s = lambda q, k, v, kb, vb: _paged(
            q.reshape(db, 1, D_QK), k.reshape(db, 1, D_QK), v.reshape(db, 1, D_VA), cache_k4, cache_v4, l,
            page_table, lam_rows, g, lam_init).reshape(1, db, D_VA)
        xs, k_new, v_new, s_fin, pr3 = _layer(xs, lw, sel512, lam_init, 1, state_shift[l][None],
                                              _to_groups(state_wkv[l]), attend_s, nf, final)
        ks_l.append(k_new.reshape(db, 1, 2 * H_ATT, DH_ATT))
        vs_l.append(v_new.reshape(db, 1, H_ATT, DV_ATT))
        ws_l.append(_from_groups(s_fin).astype(state_wkv.dtype))
        ss_l.append(pr3[0])

    y_prompt = xp[:, n_meta:t]
    y_sample = xs.reshape(db, 1, d)
    return (y_prompt, y_sample, jnp.stack(kp_l), jnp.stack(vp_l), jnp.stack(wp_l), jnp.stack(sp_l),
            jnp.stack(ks_l), jnp.stack(vs_l), jnp.stack(ws_l), jnp.stack(ss_l))
```

```python
import functools
import math

import jax
import jax.numpy as jnp
from jax import lax
from jax.experimental import pallas as pl
from jax.experimental.pallas import tpu as pltpu

F32 = jnp.float32
BF16 = jnp.bfloat16
HI = lax.Precision.HIGHEST

H_RWKV = 8
N_RWKV = 64
D_RWKV = H_RWKV * N_RWKV
LORA = 64
RWKV_COLS = 3 * D_RWKV + 2 * LORA
DECAY_SCALE = 0.606531
LNX_EPS = 64e-5
H_ATT = 4
DH_ATT = 64
DV_ATT = 2 * DH_ATT
D_QK = 2 * H_ATT * DH_ATT
D_VA = H_ATT * DV_ATT
ATT_SCALE = DH_ATT ** -0.5
NEG = -1e30
RMS_EPS = 1e-6

LANES = 128
VMEM_LIMIT = 56 * 1024 * 1024


def _cparams(sem):
    return pltpu.CompilerParams(dimension_semantics=sem, vmem_limit_bytes=VMEM_LIMIT)


def _pick(n, cands):
    for c in cands:
        if n % c == 0:
            return c
    return n


def _const_spec(shape):
    nd = len(shape)
    return pl.BlockSpec(shape, lambda *_: (0,) * nd, pipeline_mode=pl.Buffered(1))


def _seg_ones(n, seg, scale=1.0):
    i = lax.broadcasted_iota(jnp.int32, (n, n), 0) // seg
    j = lax.broadcasted_iota(jnp.int32, (n, n), 1) // seg
    return jnp.where(i == j, scale, 0.0).astype(F32)


def _inproj_kernel(x_ref, g_ref, w_ref, pr_ref, q_ref, k_ref, v_ref, kb_ref, vb_ref, gate_ref, *, d_model):
    x = x_ref[...]
    hn = x * lax.rsqrt(jnp.mean(x * x, axis=-1, keepdims=True) + RMS_EPS) * g_ref[...]
    hb = hn.astype(BF16)

    def proj(lo, n):
        return jnp.dot(hb, w_ref[:, lo:lo + n], preferred_element_type=F32)

    q0 = RWKV_COLS
    k0 = q0 + D_QK
    v0 = k0 + D_QK
    g0 = v0 + D_VA
    pr_ref[...] = proj(0, RWKV_COLS)
    q_ref[...] = (proj(q0, D_QK) * ATT_SCALE).astype(q_ref.dtype)
    k = proj(k0, D_QK)
    k_ref[...] = k
    kb_ref[...] = k.astype(BF16)
    v = proj(v0, D_VA)
    v_ref[...] = v
    vb_ref[...] = v.astype(BF16)
    gate_ref[...] = jax.nn.sigmoid(proj(g0, 2 * d_model))


def _inproj(x2, g, w_bf, q_dtype):
    m, d = x2.shape
    d_in = w_bf.shape[1]
    tm = _pick(m, (256, 128))
    row = lambda n: pl.BlockSpec((tm, n), lambda i: (i, 0))
    outs = [(RWKV_COLS, F32), (D_QK, q_dtype), (D_QK, F32), (D_VA, F32), (D_QK, BF16), (D_VA, BF16), (2 * d, F32)]
    return pl.pallas_call(
        functools.partial(_inproj_kernel, d_model=d),
        grid=(m // tm,),
        in_specs=[row(d), _const_spec((1, d)), _const_spec((d, d_in))],
        out_specs=[row(n) for n, _ in outs],
        out_shape=[jax.ShapeDtypeStruct((m, n), dt) for n, dt in outs],
        compiler_params=_cparams(("parallel",)),
        name="inproj",
    )(x2, g.reshape(1, d), w_bf)


def _prep_kernel(p_ref, prev_ref, mu_ref, w0_ref, ww2_ref, a0_ref, wa2_ref, kk_ref, ka_ref, rk_ref, sel_ref,
                 r_out, w_out, k_out, v_out, kk_out, kka_out, bonus_out, *, shift_rows):
    p = p_ref[0]
    if shift_rows:
        first = pl.program_id(1) == 0
        before = jnp.where(first, 0.0, prev_ref[0][7:8])
        rolled = pltpu.roll(p, 1, axis=0)
        rows = lax.broadcasted_iota(jnp.int32, p.shape, 0)
        prev = jnp.where(rows == 0, before, rolled)
    else:
        prev = prev_ref[0]
    m = p + (prev - p) * mu_ref[...]
    r = m[:, :D_RWKV]
    k = m[:, D_RWKV:2 * D_RWKV]
    v = m[:, 2 * D_RWKV:3 * D_RWKV]
    w_lo = m[:, 3 * D_RWKV:3 * D_RWKV + LORA]
    a_lo = m[:, 3 * D_RWKV + LORA:]
    sel = sel_ref[...]
    w = jnp.exp(-DECAY_SCALE * jax.nn.sigmoid(
        w0_ref[...] + jnp.dot(jnp.tanh(w_lo), ww2_ref[...], precision=HI, preferred_element_type=F32)))
    a = jax.nn.sigmoid(a0_ref[...] + jnp.dot(a_lo, wa2_ref[...], precision=HI, preferred_element_type=F32))
    kk = k * kk_ref[...]
    k2 = k * (1.0 + (a - 1.0) * ka_ref[...])
    ss = jnp.dot(kk * kk, sel, precision=HI, preferred_element_type=F32)
    kkn = kk / jnp.maximum(jnp.sqrt(ss), 1e-12)
    r_out[0] = r
    w_out[0] = w
    k_out[0] = k2
    v_out[0] = v
    kk_out[0] = kkn
    kka_out[0] = kkn * a
    bonus_out[0] = jnp.dot(r * k2 * rk_ref[...], sel, precision=HI, preferred_element_type=F32) * v


def _prep(p3, prev3, lw, sel512):
    b, tp, c = p3.shape
    shift_rows = prev3 is None
    tt = _pick(tp, (384, 256, 128))
    cur = pl.BlockSpec((1, tt, c), lambda i, j: (i, j, 0))
    if shift_rows:
        nb8 = tt // 8
        prev_spec = pl.BlockSpec((1, 8, c), lambda i, j: (i, jnp.maximum(j * nb8 - 1, 0), 0))
        prev_arr = p3
    else:
        prev_spec = cur
        prev_arr = prev3
    out = pl.BlockSpec((1, tt, D_RWKV), lambda i, j: (i, j, 0))
    vec = lambda n: _const_spec((1, n))
    return pl.pallas_call(
        functools.partial(_prep_kernel, shift_rows=shift_rows),
        grid=(b, tp // tt),
        in_specs=[cur, prev_spec, vec(c), vec(D_RWKV), _const_spec((LORA, D_RWKV)), vec(D_RWKV),
                  _const_spec((LORA, D_RWKV)), vec(D_RWKV), vec(D_RWKV), vec(D_RWKV),
                  _const_spec((D_RWKV, D_RWKV))],
        out_specs=[out] * 7,
        out_shape=[jax.ShapeDtypeStruct((b, tp, D_RWKV), F32)] * 7,
        compiler_params=_cparams(("parallel", "parallel")),
        name="prep",
    )(p3, prev_arr, lw['shift_mu'], lw['w0'], lw['w_w2'], lw['a0'], lw['w_a2'], lw['k_k'], lw['k_a'],
      lw['r_k'], sel512)


WKV_W = 4 * N_RWKV


def _wkv_kernel(r_ref, w_ref, k_ref, v_ref, kk_ref, kka_ref, s0_ref, y_ref, sT_ref, s_scr, ybuf, *, t_valid, nb, tc,
                gsz):
    c = pl.program_id(1)
    nq = D_RWKV // WKV_W

    @pl.when(c == 0)
    def _():
        s_scr[...] = s0_ref[...].reshape(s_scr.shape)

    nsteps = jnp.clip(t_valid - c * tc, 0, tc)

    @pl.when(nsteps < tc)
    def _():
        y_ref[...] = jnp.zeros_like(y_ref)

    sel = _seg_ones(WKV_W, N_RWKV).astype(BF16)
    eye = jnp.where(lax.broadcasted_iota(jnp.int32, (N_RWKV, WKV_W), 0)
                    == lax.broadcasted_iota(jnp.int32, (N_RWKV, WKV_W), 1) % N_RWKV, 1.0, 0.0).astype(F32)[None]
    chains = [(b, q) for b in range(nb) for q in range(nq)]

    def seg_sum(x3, pieces):
        x = x3.reshape(-1, WKV_W)
        hi = x.astype(BF16)
        if pieces == 1:
            o = jnp.dot(hi, sel, preferred_element_type=F32)
        else:
            lo = (x - hi.astype(F32)).astype(BF16)
            o = jnp.dot(jnp.concatenate([hi, lo], axis=0), sel, preferred_element_type=F32)
            o = o[:x.shape[0]] + o[x.shape[0]:]
        return o.reshape(x3.shape)

    def group(g, carry):
        t0 = pl.multiple_of(g * gsz, gsz)

        def rows(ref, i):
            return jnp.stack([ref[b, pl.ds(t0, gsz), WKV_W * q:WKV_W * (q + 1)][i:i + 1] for b, q in chains])

        for i in range(gsz):
            vb = seg_sum(eye * rows(v_ref, i), 1)
            s = s_scr[...]
            sk = seg_sum(s * rows(kk_ref, i), 2)
            s = s * rows(w_ref, i) - sk * rows(kka_ref, i) + vb * rows(k_ref, i)
            s_scr[...] = s
            yb = seg_sum(s * rows(r_ref, i), 1)
            ybuf[:, i:i + 1, :] = jnp.sum(yb * eye, axis=1, keepdims=True)
        for n, (b, q) in enumerate(chains):
            y_ref[b, pl.ds(t0, gsz), WKV_W * q:WKV_W * (q + 1)] = ybuf[n]
        return carry

    lax.fori_loop(0, nsteps // gsz, group, 0)

    @pl.when(c == pl.num_programs(1) - 1)
    def _():
        sT_ref[...] = s_scr[...].reshape(sT_ref.shape)


def _wkv(r, w, k, v, kk, kka, s0, t_valid):
    b, tp, _ = r.shape
    nq = D_RWKV // WKV_W
    nb = _pick(b, (4, 2, 1))
    tc = _pick(tp, (128,))
    gsz = 8 if tc % 8 == 0 else tc
    assert t_valid % gsz == 0, "sequence length must fill whole 8-step groups"
    seq = pl.BlockSpec((nb, tc, D_RWKV), lambda i, c: (i, c, 0))
    st = pl.BlockSpec((nb, nq, N_RWKV, WKV_W), lambda i, c: (i, 0, 0, 0))
    return pl.pallas_call(
        functools.partial(_wkv_kernel, t_valid=t_valid, nb=nb, tc=tc, gsz=gsz),
        grid=(b // nb, tp // tc),
        in_specs=[seq] * 6 + [st],
        out_specs=[seq, st],
        out_shape=[jax.ShapeDtypeStruct((b, tp, D_RWKV), F32),
                   jax.ShapeDtypeStruct((b, nq, N_RWKV, WKV_W), F32)],
        scratch_shapes=[pltpu.VMEM((nb * nq, N_RWKV, WKV_W), F32),
                        pltpu.VMEM((nb * nq, gsz, WKV_W), F32)],
        compiler_params=_cparams(("parallel", "arbitrary")),
        name="wkv",
    )(r, w, k, v, kk, kka, s0)


def _to_groups(s):
    b = s.shape[0]
    nq = D_RWKV // WKV_W
    return s.reshape(b, nq, H_RWKV // nq, N_RWKV, N_RWKV).transpose(0, 1, 3, 2, 4).reshape(b, nq, N_RWKV, WKV_W)


def _from_groups(s):
    b = s.shape[0]
    nq = D_RWKV // WKV_W
    return s.reshape(b, nq, N_RWKV, H_RWKV // nq, N_RWKV).transpose(0, 1, 3, 2, 4).reshape(
        b, H_RWKV, N_RWKV, N_RWKV)


def _lam(lq1, lk1, lq2, lk2, lam_init):
    return (jnp.exp(jnp.sum(lq1 * lk1, axis=-1, keepdims=True))
            - jnp.exp(jnp.sum(lq2 * lk2, axis=-1, keepdims=True)) + lam_init)


def _subln(o, g, lam_init):
    return o * lax.rsqrt(jnp.mean(o * o, axis=-1, keepdims=True) + RMS_EPS) * g * (1.0 - lam_init)


def _flash_kernel(q_ref, k_ref, v_ref, lq1, lk1, lq2, lk2, g_ref, o_ref, m_scr, l_scr, acc_scr, *, lam_init, tq):
    qi = pl.program_id(1)
    ki = pl.program_id(2)

    @pl.when(ki == 0)
    def _():
        m_scr[...] = jnp.full_like(m_scr, NEG)
        l_scr[...] = jnp.zeros_like(l_scr)
        acc_scr[...] = jnp.zeros_like(acc_scr)

    def block(masked):
        lane = lax.broadcasted_iota(jnp.int32, (tq, LANES), 1)
        if masked:
            keep = (lax.broadcasted_iota(jnp.int32, (tq, tq), 1) <= lax.broadcasted_iota(jnp.int32, (tq, tq), 0))
        ones = jnp.ones((tq, LANES), BF16)
        for h in range(H_ATT):
            ln = slice(LANES * h, LANES * (h + 1))
            qh = q_ref[0, :, ln]
            kh = k_ref[0, :, ln]
            v1 = jnp.concatenate([v_ref[0, :, ln], ones], axis=1)
            for sub in range(2):
                j = 2 * h + sub
                ks = jnp.where((lane < DH_ATT) if sub == 0 else (lane >= DH_ATT), kh, jnp.zeros_like(kh))
                s = lax.dot_general(qh, ks, (((1,), (1,)), ((), ())), preferred_element_type=F32)
                if masked:
                    s = jnp.where(keep, s, NEG)
                m_prev = m_scr[j]
                m_new = jnp.maximum(m_prev, jnp.max(s, axis=-1, keepdims=True))
                alpha = jnp.exp(m_prev - m_new)
                p = jnp.exp((s - jnp.concatenate([m_new] * (tq // LANES), axis=1)).astype(BF16))
                pv = jnp.dot(p, v1, preferred_element_type=F32)
                acc_scr[j] = alpha * acc_scr[j] + pv[:, :LANES]
                l_scr[j] = alpha * l_scr[j] + pv[:, LANES:]
                m_scr[j] = m_new

    @pl.when(ki < qi)
    def _():
        block(False)

    @pl.when(ki == qi)
    def _():
        block(True)
        lam = _lam(lq1[...], lk1[...], lq2[...], lk2[...], lam_init)
        for h in range(H_ATT):
            o1 = acc_scr[2 * h] / l_scr[2 * h]
            o2 = acc_scr[2 * h + 1] / l_scr[2 * h + 1]
            o_ref[0, :, LANES * h:LANES * (h + 1)] = _subln(o1 - lam * o2, g_ref[...], lam_init)


def _flash(qb, kb, vb, lam_rows, g, lam_init):
    b, tp, _ = qb.shape
    tq = _pick(tp, (384, 256, 128))
    nq = tp // tq
    qspec = pl.BlockSpec((1, tq, D_QK), lambda i, q, k: (i, q, 0))
    kspec = pl.BlockSpec((1, tq, D_QK), lambda i, q, k: (i, jnp.minimum(k, q), 0))
    vec = _const_spec((1, DH_ATT))
    return pl.pallas_call(
        functools.partial(_flash_kernel, lam_init=lam_init, tq=tq),
        grid=(b, nq, nq),
        in_specs=[qspec, kspec, kspec, vec, vec, vec, vec, _const_spec((1, DV_ATT))],
        out_specs=qspec,
        out_shape=jax.ShapeDtypeStruct((b, tp, D_VA), F32),
        scratch_shapes=[pltpu.VMEM((2 * H_ATT, tq, LANES), F32)] * 3,
        compiler_params=_cparams(("parallel", "parallel", "arbitrary")),
        name="flash",
    )(qb, kb, vb, *lam_rows, g)


PAGED_ROWS = 16


def _paged_kernel(pt_ref, q_ref, kn_ref, vn_ref, *rest, n_pages, lam_init):
    k_refs, v_refs = rest[:n_pages], rest[n_pages:2 * n_pages]
    lq1, lk1, lq2, lk2, g_ref, o_ref = rest[2 * n_pages:]
    page = v_refs[0].shape[0] // H_ATT
    rows = lax.broadcasted_iota(jnp.int32, (PAGED_ROWS, D_QK), 0)
    cols = lax.broadcasted_iota(jnp.int32, (PAGED_ROWS, D_QK), 1) // DH_ATT
    q8 = jnp.where(rows == cols, q_ref[0], 0.0)
    q_hi = q8.astype(BF16)
    q2 = jnp.concatenate([q_hi, (q8 - q_hi.astype(F32)).astype(BF16)], axis=0)
    scores = []
    for i in range(n_pages):
        kt = k_refs[i][...].reshape(D_QK, page).astype(BF16)
        s2 = jnp.dot(q2, kt, preferred_element_type=F32)
        scores.append(s2[:PAGED_ROWS] + s2[PAGED_ROWS:])
    s_self = jnp.sum(q8 * kn_ref[0], axis=-1, keepdims=True)
    m = s_self
    for s in scores:
        m = jnp.maximum(m, jnp.max(s, axis=-1, keepdims=True))
    p_self = jnp.exp(s_self - m)
    l = p_self
    acc = [p_self * vn_ref[0][:, LANES * h:LANES * (h + 1)] for h in range(H_ATT)]
    for i in range(n_pages):
        p = jnp.exp(scores[i] - m)
        l = l + jnp.sum(p, axis=-1, keepdims=True)
        pb = p.astype(BF16)
        for h in range(H_ATT):
            vh = v_refs[i][pl.ds(h, page, stride=H_ATT), :].astype(BF16)
            acc[h] = acc[h] + jnp.dot(pb, vh, preferred_element_type=F32)
    lam = _lam(lq1[...], lk1[...], lq2[...], lk2[...], lam_init)
    for h in range(H_ATT):
        o = acc[h] / l
        d = o[2 * h:2 * h + 1] - lam * o[2 * h + 1:2 * h + 2]
        o_ref[0, :, LANES * h:LANES * (h + 1)] = _subln(d, g_ref[...], lam_init)


def _paged(q, kn, vn, cache_kt, cache_v2, layer, page_table, lam_rows, g, lam_init):
    db = q.shape[0]
    n_pages = page_table.shape[1]
    page = cache_kt.shape[-1]
    tok = pl.BlockSpec((1, 1, D_QK), lambda b, pt: (b, 0, 0))
    kpg = lambda i: pl.BlockSpec((None, None, 2 * H_ATT, DH_ATT, page),
                                 lambda b, pt: (layer, pt[b * n_pages + i], 0, 0, 0))
    vpg = lambda i: pl.BlockSpec((None, None, page * H_ATT, DV_ATT), lambda b, pt: (layer, pt[b * n_pages + i], 0, 0))
    vec = pl.BlockSpec((1, DH_ATT), lambda b, pt: (0, 0))
    grid_spec = pltpu.PrefetchScalarGridSpec(
        num_scalar_prefetch=1,
        grid=(db,),
        in_specs=([tok, tok, tok] + [kpg(i) for i in range(n_pages)] + [vpg(i) for i in range(n_pages)]
                  + [vec, vec, vec, vec, pl.BlockSpec((1, DV_ATT), lambda b, pt: (0, 0))]),
        out_specs=tok,
    )
    return pl.pallas_call(
        functools.partial(_paged_kernel, n_pages=n_pages, lam_init=lam_init),
        grid_spec=grid_spec,
        out_shape=jax.ShapeDtypeStruct((db, 1, D_VA), F32),
        compiler_params=_cparams(("parallel",)),
        name="paged",
    )(page_table.reshape(-1), q, kn, vn, *([cache_kt] * n_pages), *([cache_v2] * n_pages), *lam_rows, g)


def _mix_kernel(x_ref, y_ref, bonus_ref, ob_ref, gate_ref, lng_ref, lnb_ref, sel_ref, wpa_ref, wpb_ref, wo_ref,
                nm_ref, wup_ref, wdn_ref, nf_ref, xo_ref, *, d_model, ff_chunk, final):
    sel = sel_ref[...] * (1.0 / N_RWKV)
    y = y_ref[...]
    mu = jnp.dot(y, sel, precision=HI, preferred_element_type=F32)
    d = y - mu
    var = jnp.dot(d * d, sel, precision=HI, preferred_element_type=F32)
    o_a = d * lax.rsqrt(var + LNX_EPS) * lng_ref[...] + lnb_ref[...] + bonus_ref[...]
    gates = gate_ref[...]
    merged = (gates[:, :d_model] * jnp.dot(o_a.astype(BF16), wpa_ref[...], preferred_element_type=F32)
              + gates[:, d_model:] * jnp.dot(ob_ref[...].astype(BF16), wpb_ref[...], preferred_element_type=F32))
    x = x_ref[...] + jnp.dot(merged.astype(BF16), wo_ref[...], preferred_element_type=F32)
    h2 = (x * lax.rsqrt(jnp.mean(x * x, axis=-1, keepdims=True) + RMS_EPS) * nm_ref[...]).astype(BF16)
    d_ff = wup_ref.shape[1]
    for c in range(d_ff // ff_chunk):
        cs = slice(c * ff_chunk, (c + 1) * ff_chunk)
        u = jnp.maximum(jnp.dot(h2, wup_ref[:, cs], preferred_element_type=F32), 0.0)
        x = x + jnp.dot((u * u).astype(BF16), wdn_ref[cs, :], preferred_element_type=F32)
    if final:
        x = x * lax.rsqrt(jnp.mean(x * x, axis=-1, keepdims=True) + RMS_EPS) * nf_ref[...]
    xo_ref[...] = x


def _mix(x2, y2, bonus2, ob2, gates2, lw, sel512, norm_final, final):
    m, d = x2.shape
    d_ff = lw['w_up_b'].shape[1]
    tm = _pick(m, (256, 128))
    row = lambda n: pl.BlockSpec((tm, n), lambda i: (i, 0))
    cs = _const_spec
    return pl.pallas_call(
        functools.partial(_mix_kernel, d_model=d, ff_chunk=min(d_ff, 1024), final=final),
        grid=(m // tm,),
        in_specs=[row(d), row(D_RWKV), row(D_RWKV), row(D_VA), row(2 * d), cs((1, D_RWKV)), cs((1, D_RWKV)),
                  cs((D_RWKV, D_RWKV)), cs((D_RWKV, d)), cs((D_VA, d)), cs((d, d)), cs((1, d)),
                  cs((d, d_ff)), cs((d_ff, d)), cs((1, d))],
        out_specs=row(d),
        out_shape=jax.ShapeDtypeStruct((m, d), F32),
        compiler_params=_cparams(("parallel",)),
        name="mix",
    )(x2, y2, bonus2, ob2, gates2, lw['lnx_g'], lw['lnx_b'], sel512, lw['w_pa_b'], lw['w_pb_b'], lw['w_o_b'],
      lw['norm_mlp'], lw['w_up_b'], lw['w_down_b'], norm_final)


def _layer(x3, lw, sel512, lam_init, t_valid, prev3, s0_pairs, attend, norm_final, final):
    b, tp, d = x3.shape
    m = b * tp
    pr, q, k, v, kb, vb, gates = _inproj(x3.reshape(m, d), lw['norm_mix'], lw['w_in_b'],
                                         BF16 if prev3 is None else F32)
    three = lambda a: a.reshape(b, tp, a.shape[-1])
    pr3 = three(pr)
    r, w, k2, vv, kk, kka, bonus = _prep(pr3, prev3, lw, sel512)
    if prev3 is not None:
        seqs = lambda a: a.reshape(m, 1, D_RWKV)
        y, s_fin = _wkv(seqs(r), seqs(w), seqs(k2), seqs(vv), seqs(kk), seqs(kka), s0_pairs, 1)
    else:
        y, s_fin = _wkv(r, w, k2, vv, kk, kka, s0_pairs, t_valid)
    ob = attend(three(q), three(k), three(v), three(kb), three(vb))
    xo = _mix(x3.reshape(m, d), y.reshape(m, D_RWKV), bonus.reshape(m, D_RWKV), ob.reshape(m, D_VA), gates,
              lw, sel512, norm_final, final)
    return xo.reshape(b, tp, d), three(k), three(v), s_fin, pr3


def kernel(x_prompt, x_sample, cache_k, cache_v, state_wkv, state_shift, page_table, meta_tokens, norm_mix, w_in,
           shift_mu, w0, w_w2, a0, w_a2, k_k, k_a, r_k, lnx_g, lnx_b, lambda_q1, lambda_k1, lambda_q2, lambda_k2,
           subln_g, w_pa, w_pb, w_o, norm_mlp, w_up, w_down, norm_final):
    bsz, seq, d = x_prompt.shape
    db = x_sample.shape[0]
    depth = w_in.shape[0]
    n_meta = meta_tokens.shape[0]
    t = seq + n_meta
    tp = -(-t // LANES) * LANES
    n_pool, page = cache_k.shape[1], cache_k.shape[2]

    xp = jnp.concatenate([jnp.broadcast_to(meta_tokens[None].astype(x_prompt.dtype), (bsz, n_meta, d)), x_prompt,
                          jnp.zeros((bsz, tp - t, d), x_prompt.dtype)], axis=1)
    xs = x_sample.reshape(1, db, d)
    cache_kt = cache_k.transpose(0, 1, 3, 4, 2)
    cache_v2 = cache_v.reshape(depth, n_pool, page * H_ATT, DV_ATT)
    sel512 = _seg_ones(D_RWKV, N_RWKV)
    nf = norm_final.reshape(1, d)
    zero_state = jnp.zeros((bsz, D_RWKV // WKV_W, N_RWKV, WKV_W), F32)

    kp_l, vp_l, wp_l, sp_l, ks_l, vs_l, ws_l, ss_l = ([] for _ in range(8))
    for l in range(depth):
        row = lambda a: a[l].reshape(1, -1)
        lw = {
            'norm_mix': norm_mix[l], 'w_in_b': w_in[l].astype(BF16), 'shift_mu': row(shift_mu), 'w0': row(w0),
            'w_w2': w_w2[l], 'a0': row(a0), 'w_a2': w_a2[l], 'k_k': row(k_k), 'k_a': row(k_a), 'r_k': row(r_k),
            'lnx_g': row(lnx_g), 'lnx_b': row(lnx_b), 'w_pa_b': w_pa[l].astype(BF16),
            'w_pb_b': w_pb[l].astype(BF16), 'w_o_b': w_o[l].astype(BF16), 'norm_mlp': row(norm_mlp),
            'w_up_b': w_up[l].astype(BF16), 'w_down_b': w_down[l].astype(BF16),
        }
        lam_init = 0.8 - 0.6 * math.exp(-0.3 * l)
        lam_rows = (row(lambda_q1), row(lambda_k1), row(lambda_q2), row(lambda_k2))
        g = row(subln_g)
        final = l == depth - 1

        attend_p = lambda q, k, v, kb, vb: _flash(q, kb, vb, lam_rows, g, lam_init)
        xp, k_new, v_new, s_fin, pr3 = _layer(xp, lw, sel512, lam_init, t, None, zero_state, attend_p, nf, final)
        kp_l.append(k_new[:, :t].reshape(bsz, t, 2 * H_ATT, DH_ATT))
        vp_l.append(v_new[:, :t].reshape(bsz, t, H_ATT, DV_ATT))
        wp_l.append(_from_groups(s_fin))
        sp_l.append(pr3[:, t - 1])

        attend_s = lambda q, k, v, kb, vb: _paged(
            q.reshape(db, 1, D_QK), k.reshape(db, 1, D_QK), v.reshape(db, 1, D_VA), cache_kt, cache_v2, l,
            page_table, lam_rows, g, lam_init).reshape(1, db, D_VA)
        xs, k_new, v_new, s_fin, pr3 = _layer(xs, lw, sel512, lam_init, 1, state_shift[l][None],
                                              _to_groups(state_wkv[l]), attend_s, nf, final)
        ks_l.append(k_new.reshape(db, 1, 2 * H_ATT, DH_ATT))
        vs_l.append(v_new.reshape(db, 1, H_ATT, DV_ATT))
        ws_l.append(_from_groups(s_fin).astype(state_wkv.dtype))
        ss_l.append(pr3[0])

    y_prompt = xp[:, n_meta:t]
    y_sample = xs.reshape(db, 1, d)
    return (y_prompt, y_sample, jnp.stack(kp_l), jnp.stack(vp_l), jnp.stack(wp_l), jnp.stack(sp_l),
            jnp.stack(ks_l), jnp.stack(vs_l), jnp.stack(ws_l), jnp.stack(ss_l))
```

```python
import functools
import math

import jax
import jax.numpy as jnp
from jax import lax
from jax.experimental import pallas as pl
from jax.experimental.pallas import tpu as pltpu

F32 = jnp.float32
BF16 = jnp.bfloat16
HI = lax.Precision.HIGHEST

H_RWKV = 8
N_RWKV = 64
D_RWKV = H_RWKV * N_RWKV
LORA = 64
RWKV_COLS = 3 * D_RWKV + 2 * LORA
DECAY_SCALE = 0.606531
LNX_EPS = 64e-5
H_ATT = 4
DH_ATT = 64
DV_ATT = 2 * DH_ATT
D_QK = 2 * H_ATT * DH_ATT
D_VA = H_ATT * DV_ATT
ATT_SCALE = DH_ATT ** -0.5
NEG = -1e30
RMS_EPS = 1e-6

LANES = 128
VMEM_LIMIT = 56 * 1024 * 1024


def _cparams(sem):
    return pltpu.CompilerParams(dimension_semantics=sem, vmem_limit_bytes=VMEM_LIMIT)


def _pick(n, cands):
    for c in cands:
        if n % c == 0:
            return c
    return n


def _const_spec(shape):
    nd = len(shape)
    return pl.BlockSpec(shape, lambda *_: (0,) * nd, pipeline_mode=pl.Buffered(1))


def _seg_ones(n, seg, scale=1.0):
    i = lax.broadcasted_iota(jnp.int32, (n, n), 0) // seg
    j = lax.broadcasted_iota(jnp.int32, (n, n), 1) // seg
    return jnp.where(i == j, scale, 0.0).astype(F32)


def _seg_sum2(x, sel_b):
    hi = x.astype(BF16)
    lo = (x - hi.astype(F32)).astype(BF16)
    o = jnp.dot(jnp.concatenate([hi, lo], axis=0), sel_b, preferred_element_type=F32)
    return o[:x.shape[0]] + o[x.shape[0]:]


def _inproj_kernel(x_ref, g_ref, w_ref, pr_ref, q_ref, k_ref, v_ref, kb_ref, vb_ref, gate_ref, *, d_model):
    x = x_ref[...]
    hn = x * lax.rsqrt(jnp.mean(x * x, axis=-1, keepdims=True) + RMS_EPS) * g_ref[...]
    hb = hn.astype(BF16)

    def proj(lo, n):
        return jnp.dot(hb, w_ref[:, lo:lo + n], preferred_element_type=F32)

    q0 = RWKV_COLS
    k0 = q0 + D_QK
    v0 = k0 + D_QK
    g0 = v0 + D_VA
    pr_ref[...] = proj(0, RWKV_COLS)
    q_ref[...] = (proj(q0, D_QK) * ATT_SCALE).astype(q_ref.dtype)
    k = proj(k0, D_QK)
    k_ref[...] = k
    kb_ref[...] = k.astype(BF16)
    v = proj(v0, D_VA)
    v_ref[...] = v
    vb_ref[...] = v.astype(BF16)
    gate_ref[...] = jax.nn.sigmoid(proj(g0, 2 * d_model))


def _inproj(x2, g, w_bf, q_dtype):
    m, d = x2.shape
    d_in = w_bf.shape[1]
    tm = _pick(m, (256, 128))
    row = lambda n: pl.BlockSpec((tm, n), lambda i: (i, 0))
    outs = [(RWKV_COLS, F32), (D_QK, q_dtype), (D_QK, F32), (D_VA, F32), (D_QK, BF16), (D_VA, BF16), (2 * d, F32)]
    return pl.pallas_call(
        functools.partial(_inproj_kernel, d_model=d),
        grid=(m // tm,),
        in_specs=[row(d), _const_spec((1, d)), _const_spec((d, d_in))],
        out_specs=[row(n) for n, _ in outs],
        out_shape=[jax.ShapeDtypeStruct((m, n), dt) for n, dt in outs],
        compiler_params=_cparams(("parallel",)),
        name="inproj",
    )(x2, g.reshape(1, d), w_bf)


def _prep_kernel(p_ref, prev_ref, mu_ref, w0_ref, ww2_ref, a0_ref, wa2_ref, kk_ref, ka_ref, rk_ref, sel_ref,
                 r_out, w_out, k_out, v_out, kk_out, kka_out, bonus_out, *, shift_rows):
    p = p_ref[0]
    if shift_rows:
        first = pl.program_id(1) == 0
        before = jnp.where(first, 0.0, prev_ref[0][7:8])
        rolled = pltpu.roll(p, 1, axis=0)
        rows = lax.broadcasted_iota(jnp.int32, p.shape, 0)
        prev = jnp.where(rows == 0, before, rolled)
    else:
        prev = prev_ref[0]
    m = p + (prev - p) * mu_ref[...]
    r = m[:, :D_RWKV]
    k = m[:, D_RWKV:2 * D_RWKV]
    v = m[:, 2 * D_RWKV:3 * D_RWKV]
    w_lo = m[:, 3 * D_RWKV:3 * D_RWKV + LORA]
    a_lo = m[:, 3 * D_RWKV + LORA:]
    sel = sel_ref[...]
    w = jnp.exp(-DECAY_SCALE * jax.nn.sigmoid(
        w0_ref[...] + jnp.dot(jnp.tanh(w_lo), ww2_ref[...], precision=HI, preferred_element_type=F32)))
    a = jax.nn.sigmoid(a0_ref[...] + jnp.dot(a_lo, wa2_ref[...], precision=HI, preferred_element_type=F32))
    kk = k * kk_ref[...]
    k2 = k * (1.0 + (a - 1.0) * ka_ref[...])
    ss = _seg_sum2(kk * kk, sel)
    kkn = kk / jnp.maximum(jnp.sqrt(ss), 1e-12)
    r_out[0] = r
    w_out[0] = w
    k_out[0] = k2
    v_out[0] = v
    kk_out[0] = kkn
    kka_out[0] = kkn * a
    bonus_out[0] = _seg_sum2(r * k2 * rk_ref[...], sel) * v


def _prep(p3, prev3, lw, sel512):
    b, tp, c = p3.shape
    shift_rows = prev3 is None
    tt = _pick(tp, (384, 256, 128))
    cur = pl.BlockSpec((1, tt, c), lambda i, j: (i, j, 0))
    if shift_rows:
        nb8 = tt // 8
        prev_spec = pl.BlockSpec((1, 8, c), lambda i, j: (i, jnp.maximum(j * nb8 - 1, 0), 0))
        prev_arr = p3
    else:
        prev_spec = cur
        prev_arr = prev3
    out = pl.BlockSpec((1, tt, D_RWKV), lambda i, j: (i, j, 0))
    vec = lambda n: _const_spec((1, n))
    return pl.pallas_call(
        functools.partial(_prep_kernel, shift_rows=shift_rows),
        grid=(b, tp // tt),
        in_specs=[cur, prev_spec, vec(c), vec(D_RWKV), _const_spec((LORA, D_RWKV)), vec(D_RWKV),
                  _const_spec((LORA, D_RWKV)), vec(D_RWKV), vec(D_RWKV), vec(D_RWKV),
                  _const_spec((D_RWKV, D_RWKV))],
        out_specs=[out] * 7,
        out_shape=[jax.ShapeDtypeStruct((b, tp, D_RWKV), F32)] * 7,
        compiler_params=_cparams(("parallel", "parallel")),
        name="prep",
    )(p3, prev_arr, lw['shift_mu'], lw['w0'], lw['w_w2'], lw['a0'], lw['w_a2'], lw['k_k'], lw['k_a'],
      lw['r_k'], sel512)


WKV_W = 4 * N_RWKV


def _wkv_kernel(r_ref, w_ref, k_ref, v_ref, kk_ref, kka_ref, s0_ref, y_ref, sT_ref, s_scr, ybuf, vb_scr, *, t_valid,
                nb, tc, gsz):
    c = pl.program_id(1)
    nq = D_RWKV // WKV_W

    @pl.when(c == 0)
    def _():
        s_scr[...] = s0_ref[...].reshape(s_scr.shape)

    nsteps = jnp.clip(t_valid - c * tc, 0, tc)

    @pl.when(nsteps < tc)
    def _():
        y_ref[...] = jnp.zeros_like(y_ref)

    sel = _seg_ones(WKV_W, N_RWKV).astype(BF16)
    eye = jnp.where(lax.broadcasted_iota(jnp.int32, (N_RWKV, WKV_W), 0)
                    == lax.broadcasted_iota(jnp.int32, (N_RWKV, WKV_W), 1) % N_RWKV, 1.0, 0.0).astype(F32)[None]
    eye_b = eye.astype(BF16)
    lane8 = lax.broadcasted_iota(jnp.int32, (8, WKV_W), 1)
    lane_blk = (lane8 % N_RWKV) // 8
    eye8 = jnp.where(lax.broadcasted_iota(jnp.int32, (8, WKV_W), 0) == lane8 % 8, 1.0, 0.0).astype(F32)
    chains = [(b, q) for b in range(nb) for q in range(nq)]

    nch = len(chains)
    tile = (nch, N_RWKV, WKV_W)

    def seg_sums(*xs):
        lhs = jnp.concatenate([x.astype(BF16).reshape(-1, WKV_W) for x in xs], axis=0)
        o = jnp.dot(lhs, sel, preferred_element_type=F32)
        n = nch * N_RWKV
        return [o[j * n:(j + 1) * n].reshape(tile) for j in range(len(xs))]

    def group(g, carry):
        t0 = pl.multiple_of(g * gsz, gsz)

        def rows(ref, i):
            return jnp.stack([ref[b, pl.ds(t0, gsz), WKV_W * q:WKV_W * (q + 1)][i:i + 1] for b, q in chains])

        vbs = seg_sums(*[rows(v_ref, i).astype(BF16) * eye_b for i in range(gsz)])
        for i in range(gsz):
            vb_scr[i] = vbs[i]
        s = s_scr[...]
        sk, = seg_sums(s * rows(kk_ref, 0))
        for i in range(gsz):
            s = s * rows(w_ref, i) - sk * rows(kka_ref, i) + vb_scr[i] * rows(k_ref, i)
            if i + 1 < gsz:
                sk, yb = seg_sums(s * rows(kk_ref, i + 1), s * rows(r_ref, i))
            else:
                yb, = seg_sums(s * rows(r_ref, i))
            yb4 = yb.reshape(nch, N_RWKV // 8, 8, WKV_W)
            diag = yb4[:, 0]
            for blk in range(1, N_RWKV // 8):
                diag = jnp.where(lane_blk == blk, yb4[:, blk], diag)
            ybuf[:, i:i + 1, :] = jnp.sum(diag * eye8, axis=1, keepdims=True)
        s_scr[...] = s
        for n, (b, q) in enumerate(chains):
            y_ref[b, pl.ds(t0, gsz), WKV_W * q:WKV_W * (q + 1)] = ybuf[n]
        return carry

    lax.fori_loop(0, nsteps // gsz, group, 0)

    @pl.when(c == pl.num_programs(1) - 1)
    def _():
        sT_ref[...] = s_scr[...].reshape(sT_ref.shape)


def _wkv(r, w, k, v, kk, kka, s0, t_valid):
    b, tp, _ = r.shape
    nq = D_RWKV // WKV_W
    nb = _pick(b, (4, 2, 1))
    tc = _pick(tp, (128,))
    gsz = 8 if tc % 8 == 0 else tc
    assert t_valid % gsz == 0, "sequence length must fill whole 8-step groups"
    seq = pl.BlockSpec((nb, tc, D_RWKV), lambda i, c: (i, c, 0))
    st = pl.BlockSpec((nb, nq, N_RWKV, WKV_W), lambda i, c: (i, 0, 0, 0))
    return pl.pallas_call(
        functools.partial(_wkv_kernel, t_valid=t_valid, nb=nb, tc=tc, gsz=gsz),
        grid=(b // nb, tp // tc),
        in_specs=[seq] * 6 + [st],
        out_specs=[seq, st],
        out_shape=[jax.ShapeDtypeStruct((b, tp, D_RWKV), F32),
                   jax.ShapeDtypeStruct((b, nq, N_RWKV, WKV_W), F32)],
        scratch_shapes=[pltpu.VMEM((nb * nq, N_RWKV, WKV_W), F32),
                        pltpu.VMEM((nb * nq, gsz, WKV_W), F32),
                        pltpu.VMEM((gsz, nb * nq, N_RWKV, WKV_W), F32)],
        compiler_params=_cparams(("parallel", "arbitrary")),
        name="wkv",
    )(r, w, k, v, kk, kka, s0)


def _to_groups(s):
    b = s.shape[0]
    nq = D_RWKV // WKV_W
    return s.reshape(b, nq, H_RWKV // nq, N_RWKV, N_RWKV).transpose(0, 1, 3, 2, 4).reshape(b, nq, N_RWKV, WKV_W)


def _from_groups(s):
    b = s.shape[0]
    nq = D_RWKV // WKV_W
    return s.reshape(b, nq, N_RWKV, H_RWKV // nq, N_RWKV).transpose(0, 1, 3, 2, 4).reshape(
        b, H_RWKV, N_RWKV, N_RWKV)


def _lam(lq1, lk1, lq2, lk2, lam_init):
    return (jnp.exp(jnp.sum(lq1 * lk1, axis=-1, keepdims=True))
            - jnp.exp(jnp.sum(lq2 * lk2, axis=-1, keepdims=True)) + lam_init)


def _subln(o, g, lam_init):
    return o * lax.rsqrt(jnp.mean(o * o, axis=-1, keepdims=True) + RMS_EPS) * g * (1.0 - lam_init)


def _flash_kernel(q_ref, k_ref, v_ref, lq1, lk1, lq2, lk2, g_ref, o_ref, q2_scr, m_scr, l_scr, acc_scr, *, lam_init,
                  tq):
    qi = pl.program_id(1)
    m_scr[...] = jnp.full_like(m_scr, NEG)
    l_scr[...] = jnp.zeros_like(l_scr)
    acc_scr[...] = jnp.zeros_like(acc_scr)
    lane = lax.broadcasted_iota(jnp.int32, (tq, LANES), 1)
    for h in range(H_ATT):
        qh = q_ref[0, :, LANES * h:LANES * (h + 1)]
        q2_scr[h, :tq] = jnp.where(lane < DH_ATT, qh, jnp.zeros_like(qh))
        q2_scr[h, tq:] = jnp.where(lane >= DH_ATT, qh, jnp.zeros_like(qh))

    def block(k0, masked):
        if masked:
            tri = lax.broadcasted_iota(jnp.int32, (tq, tq), 1) <= lax.broadcasted_iota(jnp.int32, (tq, tq), 0)
            keep = jnp.concatenate([tri, tri], axis=0)
        ones = jnp.ones((tq, LANES), BF16)
        for h in range(H_ATT):
            ln = slice(LANES * h, LANES * (h + 1))
            kh = k_ref[0, pl.ds(k0, tq), ln]
            v1 = jnp.concatenate([v_ref[0, pl.ds(k0, tq), ln], ones], axis=1)
            s = lax.dot_general(q2_scr[h], kh, (((1,), (1,)), ((), ())), preferred_element_type=F32)
            if masked:
                s = jnp.where(keep, s, NEG)
            m_prev = m_scr[h]
            m_new = jnp.maximum(m_prev, jnp.max(s, axis=-1, keepdims=True))
            alpha = jnp.exp(m_prev - m_new)
            p = jnp.exp((s - jnp.concatenate([m_new] * (tq // LANES), axis=1)).astype(BF16))
            pv = jnp.dot(p, v1, preferred_element_type=F32)
            acc_scr[h] = alpha * acc_scr[h] + pv[:, :LANES]
            l_scr[h] = alpha * l_scr[h] + pv[:, LANES:]
            m_scr[h] = m_new

    def body(ki, carry):
        block(pl.multiple_of(ki * tq, tq), False)
        return carry

    lax.fori_loop(0, qi, body, 0)
    block(pl.multiple_of(qi * tq, tq), True)
    lam = _lam(lq1[...], lk1[...], lq2[...], lk2[...], lam_init)
    for h in range(H_ATT):
        o = acc_scr[h] / l_scr[h]
        o_ref[0, :, LANES * h:LANES * (h + 1)] = _subln(o[:tq] - lam * o[tq:], g_ref[...], lam_init)


def _flash(qb, kb, vb, lam_rows, g, lam_init):
    b, tp, _ = qb.shape
    tq = _pick(tp, (384, 256, 128))
    qspec = pl.BlockSpec((1, tq, D_QK), lambda i, q: (i, q, 0))
    kspec = pl.BlockSpec((1, tp, D_QK), lambda i, q: (i, 0, 0))
    vec = _const_spec((1, DH_ATT))
    return pl.pallas_call(
        functools.partial(_flash_kernel, lam_init=lam_init, tq=tq),
        grid=(b, tp // tq),
        in_specs=[qspec, kspec, kspec, vec, vec, vec, vec, _const_spec((1, DV_ATT))],
        out_specs=qspec,
        out_shape=jax.ShapeDtypeStruct((b, tp, D_VA), F32),
        scratch_shapes=[pltpu.VMEM((H_ATT, 2 * tq, LANES), BF16)] + [pltpu.VMEM((H_ATT, 2 * tq, LANES), F32)] * 3,
        compiler_params=_cparams(("parallel", "arbitrary")),
        name="flash",
    )(qb, kb, vb, *lam_rows, g)


PAGED_ROWS = 16


def _paged_kernel(pt_ref, q_ref, kn_ref, vn_ref, *rest, n_pages, lam_init):
    k_refs, v_refs = rest[:n_pages], rest[n_pages:2 * n_pages]
    lq1, lk1, lq2, lk2, g_ref, o_ref = rest[2 * n_pages:]
    page = v_refs[0].shape[0] // H_ATT
    rows = lax.broadcasted_iota(jnp.int32, (PAGED_ROWS, D_QK), 0)
    cols = lax.broadcasted_iota(jnp.int32, (PAGED_ROWS, D_QK), 1) // DH_ATT
    q8 = jnp.where(rows == cols, q_ref[0], 0.0)
    q_hi = q8.astype(BF16)
    q2 = jnp.concatenate([q_hi, (q8 - q_hi.astype(F32)).astype(BF16)], axis=0)
    scores = []
    for i in range(n_pages):
        kt = k_refs[i][...].reshape(D_QK, page).astype(BF16)
        s2 = jnp.dot(q2, kt, preferred_element_type=F32)
        scores.append(s2[:PAGED_ROWS] + s2[PAGED_ROWS:])
    s_self = jnp.sum(q8 * kn_ref[0], axis=-1, keepdims=True)
    m = s_self
    for s in scores:
        m = jnp.maximum(m, jnp.max(s, axis=-1, keepdims=True))
    p_self = jnp.exp(s_self - m)
    l = p_self
    acc = [p_self * vn_ref[0][:, LANES * h:LANES * (h + 1)] for h in range(H_ATT)]
    for i in range(n_pages):
        p = jnp.exp(scores[i] - m)
        l = l + jnp.sum(p, axis=-1, keepdims=True)
        pb = p.astype(BF16)
        for h in range(H_ATT):
            vh = v_refs[i][pl.ds(h, page, stride=H_ATT), :].astype(BF16)
            acc[h] = acc[h] + jnp.dot(pb, vh, preferred_element_type=F32)
    lam = _lam(lq1[...], lk1[...], lq2[...], lk2[...], lam_init)
    for h in range(H_ATT):
        o = acc[h] / l
        d = o[2 * h:2 * h + 1] - lam * o[2 * h + 1:2 * h + 2]
        o_ref[0, :, LANES * h:LANES * (h + 1)] = _subln(d, g_ref[...], lam_init)


def _paged(q, kn, vn, cache_kt, cache_v2, layer, page_table, lam_rows, g, lam_init):
    db = q.shape[0]
    n_pages = page_table.shape[1]
    page = cache_kt.shape[-1]
    tok = pl.BlockSpec((1, 1, D_QK), lambda b, pt: (b, 0, 0))
    kpg = lambda i: pl.BlockSpec((None, None, 2 * H_ATT, DH_ATT, page),
                                 lambda b, pt: (layer, pt[b * n_pages + i], 0, 0, 0))
    vpg = lambda i: pl.BlockSpec((None, None, page * H_ATT, DV_ATT), lambda b, pt: (layer, pt[b * n_pages + i], 0, 0))
    vec = pl.BlockSpec((1, DH_ATT), lambda b, pt: (0, 0))
    grid_spec = pltpu.PrefetchScalarGridSpec(
        num_scalar_prefetch=1,
        grid=(db,),
        in_specs=([tok, tok, tok] + [kpg(i) for i in range(n_pages)] + [vpg(i) for i in range(n_pages)]
                  + [vec, vec, vec, vec, pl.BlockSpec((1, DV_ATT), lambda b, pt: (0, 0))]),
        out_specs=tok,
    )
    return pl.pallas_call(
        functools.partial(_paged_kernel, n_pages=n_pages, lam_init=lam_init),
        grid_spec=grid_spec,
        out_shape=jax.ShapeDtypeStruct((db, 1, D_VA), F32),
        compiler_params=_cparams(("parallel",)),
        name="paged",
    )(page_table.reshape(-1), q, kn, vn, *([cache_kt] * n_pages), *([cache_v2] * n_pages), *lam_rows, g)


def _mix_kernel(x_ref, y_ref, bonus_ref, ob_ref, gate_ref, lng_ref, lnb_ref, sel_ref, wpa_ref, wpb_ref, wo_ref,
                nm_ref, wup_ref, wdn_ref, nf_ref, xo_ref, *, d_model, ff_chunk, final):
    sel = sel_ref[...]
    y = y_ref[...]
    mu = _seg_sum2(y, sel) * (1.0 / N_RWKV)
    d = y - mu
    var = _seg_sum2(d * d, sel) * (1.0 / N_RWKV)
    o_a = d * lax.rsqrt(var + LNX_EPS) * lng_ref[...] + lnb_ref[...] + bonus_ref[...]
    gates = gate_ref[...]
    merged = (gates[:, :d_model] * jnp.dot(o_a.astype(BF16), wpa_ref[...], preferred_element_type=F32)
              + gates[:, d_model:] * jnp.dot(ob_ref[...].astype(BF16), wpb_ref[...], preferred_element_type=F32))
    x = x_ref[...] + jnp.dot(merged.astype(BF16), wo_ref[...], preferred_element_type=F32)
    h2 = (x * lax.rsqrt(jnp.mean(x * x, axis=-1, keepdims=True) + RMS_EPS) * nm_ref[...]).astype(BF16)
    d_ff = wup_ref.shape[1]
    for c in range(d_ff // ff_chunk):
        cs = slice(c * ff_chunk, (c + 1) * ff_chunk)
        u = jnp.maximum(jnp.dot(h2, wup_ref[:, cs], preferred_element_type=F32), 0.0)
        x = x + jnp.dot((u * u).astype(BF16), wdn_ref[cs, :], preferred_element_type=F32)
    if final:
        x = x * lax.rsqrt(jnp.mean(x * x, axis=-1, keepdims=True) + RMS_EPS) * nf_ref[...]
    xo_ref[...] = x


def _mix(x2, y2, bonus2, ob2, gates2, lw, sel512, norm_final, final):
    m, d = x2.shape
    d_ff = lw['w_up_b'].shape[1]
    tm = _pick(m, (256, 128))
    row = lambda n: pl.BlockSpec((tm, n), lambda i: (i, 0))
    cs = _const_spec
    return pl.pallas_call(
        functools.partial(_mix_kernel, d_model=d, ff_chunk=min(d_ff, 1024), final=final),
        grid=(m // tm,),
        in_specs=[row(d), row(D_RWKV), row(D_RWKV), row(D_VA), row(2 * d), cs((1, D_RWKV)), cs((1, D_RWKV)),
                  cs((D_RWKV, D_RWKV)), cs((D_RWKV, d)), cs((D_VA, d)), cs((d, d)), cs((1, d)),
                  cs((d, d_ff)), cs((d_ff, d)), cs((1, d))],
        out_specs=row(d),
        out_shape=jax.ShapeDtypeStruct((m, d), F32),
        compiler_params=_cparams(("parallel",)),
        name="mix",
    )(x2, y2, bonus2, ob2, gates2, lw['lnx_g'], lw['lnx_b'], sel512, lw['w_pa_b'], lw['w_pb_b'], lw['w_o_b'],
      lw['norm_mlp'], lw['w_up_b'], lw['w_down_b'], norm_final)


def _layer(x3, lw, sel512, lam_init, t_valid, prev3, s0_pairs, attend, norm_final, final):
    b, tp, d = x3.shape
    m = b * tp
    pr, q, k, v, kb, vb, gates = _inproj(x3.reshape(m, d), lw['norm_mix'], lw['w_in_b'],
                                         BF16 if prev3 is None else F32)
    three = lambda a: a.reshape(b, tp, a.shape[-1])
    pr3 = three(pr)
    r, w, k2, vv, kk, kka, bonus = _prep(pr3, prev3, lw, sel512)
    if prev3 is not None:
        seqs = lambda a: a.reshape(m, 1, D_RWKV)
        y, s_fin = _wkv(seqs(r), seqs(w), seqs(k2), seqs(vv), seqs(kk), seqs(kka), s0_pairs, 1)
    else:
        y, s_fin = _wkv(r, w, k2, vv, kk, kka, s0_pairs, t_valid)
    ob = attend(three(q), three(k), three(v), three(kb), three(vb))
    xo = _mix(x3.reshape(m, d), y.reshape(m, D_RWKV), bonus.reshape(m, D_RWKV), ob.reshape(m, D_VA), gates,
              lw, sel512, norm_final, final)
    return xo.reshape(b, tp, d), three(k), three(v), s_fin, pr3


def kernel(x_prompt, x_sample, cache_k, cache_v, state_wkv, state_shift, page_table, meta_tokens, norm_mix, w_in,
           shift_mu, w0, w_w2, a0, w_a2, k_k, k_a, r_k, lnx_g, lnx_b, lambda_q1, lambda_k1, lambda_q2, lambda_k2,
           subln_g, w_pa, w_pb, w_o, norm_mlp, w_up, w_down, norm_final):
    bsz, seq, d = x_prompt.shape
    db = x_sample.shape[0]
    depth = w_in.shape[0]
    n_meta = meta_tokens.shape[0]
    t = seq + n_meta
    tp = -(-t // LANES) * LANES
    n_pool, page = cache_k.shape[1], cache_k.shape[2]

    xp = jnp.concatenate([jnp.broadcast_to(meta_tokens[None].astype(x_prompt.dtype), (bsz, n_meta, d)), x_prompt,
                          jnp.zeros((bsz, tp - t, d), x_prompt.dtype)], axis=1)
    xs = x_sample.reshape(1, db, d)
    cache_kt = cache_k.transpose(0, 1, 3, 4, 2)
    cache_v2 = cache_v.reshape(depth, n_pool, page * H_ATT, DV_ATT)
    sel512 = _seg_ones(D_RWKV, N_RWKV).astype(BF16)
    nf = norm_final.reshape(1, d)
    zero_state = jnp.zeros((bsz, D_RWKV // WKV_W, N_RWKV, WKV_W), F32)

    kp_l, vp_l, wp_l, sp_l, ks_l, vs_l, ws_l, ss_l = ([] for _ in range(8))
    for l in range(depth):
        row = lambda a: a[l].reshape(1, -1)
        lw = {
            'norm_mix': norm_mix[l], 'w_in_b': w_in[l].astype(BF16), 'shift_mu': row(shift_mu), 'w0': row(w0),
            'w_w2': w_w2[l], 'a0': row(a0), 'w_a2': w_a2[l], 'k_k': row(k_k), 'k_a': row(k_a), 'r_k': row(r_k),
            'lnx_g': row(lnx_g), 'lnx_b': row(lnx_b), 'w_pa_b': w_pa[l].astype(BF16),
            'w_pb_b': w_pb[l].astype(BF16), 'w_o_b': w_o[l].astype(BF16), 'norm_mlp': row(norm_mlp),
            'w_up_b': w_up[l].astype(BF16), 'w_down_b': w_down[l].astype(BF16),
        }
        lam_init = 0.8 - 0.6 * math.exp(-0.3 * l)
        lam_rows = (row(lambda_q1), row(lambda_k1), row(lambda_q2), row(lambda_k2))
        g = row(subln_g)
        final = l == depth - 1

        attend_p = lambda q, k, v, kb, vb: _flash(q, kb, vb, lam_rows, g, lam_init)
        xp, k_new, v_new, s_fin, pr3 = _layer(xp, lw, sel512, lam_init, t, None, zero_state, attend_p, nf, final)
        kp_l.append(k_new[:, :t].reshape(bsz, t, 2 * H_ATT, DH_ATT))
        vp_l.append(v_new[:, :t].reshape(bsz, t, H_ATT, DV_ATT))
        wp_l.append(_from_groups(s_fin))
        sp_l.append(pr3[:, t - 1])

        attend_s = lambda q, k, v, kb, vb: _paged(
            q.reshape(db, 1, D_QK), k.reshape(db, 1, D_QK), v.reshape(db, 1, D_VA), cache_kt, cache_v2, l,
            page_table, lam_rows, g, lam_init).reshape(1, db, D_VA)
        xs, k_new, v_new, s_fin, pr3 = _layer(xs, lw, sel512, lam_init, 1, state_shift[l][None],
                                              _to_groups(state_wkv[l]), attend_s, nf, final)
        ks_l.append(k_new.reshape(db, 1, 2 * H_ATT, DH_ATT))
        vs_l.append(v_new.reshape(db, 1, H_ATT, DV_ATT))
        ws_l.append(_from_groups(s_fin).astype(state_wkv.dtype))
        ss_l.append(pr3[0])

    y_prompt = xp[:, n_meta:t]
    y_sample = xs.reshape(db, 1, d)
    return (y_prompt, y_sample, jnp.stack(kp_l), jnp.stack(vp_l), jnp.stack(wp_l), jnp.stack(sp_l),
            jnp.stack(ks_l), jnp.stack(vs_l), jnp.stack(ws_l), jnp.stack(ss_l))
```

```python
import functools
import math

import jax
import jax.numpy as jnp
from jax import lax
from jax.experimental import pallas as pl
from jax.experimental.pallas import tpu as pltpu

F32 = jnp.float32
BF16 = jnp.bfloat16
HI = lax.Precision.HIGHEST

H_RWKV = 8
N_RWKV = 64
D_RWKV = H_RWKV * N_RWKV
LORA = 64
RWKV_COLS = 3 * D_RWKV + 2 * LORA
DECAY_SCALE = 0.606531
LNX_EPS = 64e-5
H_ATT = 4
DH_ATT = 64
DV_ATT = 2 * DH_ATT
D_QK = 2 * H_ATT * DH_ATT
D_VA = H_ATT * DV_ATT
ATT_SCALE = DH_ATT ** -0.5
NEG = -1e30
RMS_EPS = 1e-6

LANES = 128
VMEM_LIMIT = 56 * 1024 * 1024


def _cparams(sem):
    return pltpu.CompilerParams(dimension_semantics=sem, vmem_limit_bytes=VMEM_LIMIT)


def _pick(n, cands):
    for c in cands:
        if n % c == 0:
            return c
    return n


def _const_spec(shape):
    nd = len(shape)
    return pl.BlockSpec(shape, lambda *_: (0,) * nd, pipeline_mode=pl.Buffered(1))


def _seg_ones(n, seg, scale=1.0):
    i = lax.broadcasted_iota(jnp.int32, (n, n), 0) // seg
    j = lax.broadcasted_iota(jnp.int32, (n, n), 1) // seg
    return jnp.where(i == j, scale, 0.0).astype(F32)


def _seg_sum2(x, sel_b):
    hi = x.astype(BF16)
    lo = (x - hi.astype(F32)).astype(BF16)
    o = jnp.dot(jnp.concatenate([hi, lo], axis=0), sel_b, preferred_element_type=F32)
    return o[:x.shape[0]] + o[x.shape[0]:]


def _inproj_kernel(x_ref, g_ref, w_ref, pr_ref, q_ref, k_ref, v_ref, kb_ref, vb_ref, gate_ref, *, d_model):
    x = x_ref[...]
    hn = x * lax.rsqrt(jnp.mean(x * x, axis=-1, keepdims=True) + RMS_EPS) * g_ref[...]
    hb = hn.astype(BF16)

    def proj(lo, n):
        return jnp.dot(hb, w_ref[:, lo:lo + n], preferred_element_type=F32)

    q0 = RWKV_COLS
    k0 = q0 + D_QK
    v0 = k0 + D_QK
    g0 = v0 + D_VA
    pr_ref[...] = proj(0, RWKV_COLS)
    q_ref[...] = (proj(q0, D_QK) * ATT_SCALE).astype(q_ref.dtype)
    k = proj(k0, D_QK)
    k_ref[...] = k
    kb_ref[...] = k.astype(BF16)
    v = proj(v0, D_VA)
    v_ref[...] = v
    vb_ref[...] = v.astype(BF16)
    gate_ref[...] = jax.nn.sigmoid(proj(g0, 2 * d_model))


def _inproj(x2, g, w_bf, q_dtype):
    m, d = x2.shape
    d_in = w_bf.shape[1]
    tm = _pick(m, (256, 128))
    row = lambda n: pl.BlockSpec((tm, n), lambda i: (i, 0))
    outs = [(RWKV_COLS, F32), (D_QK, q_dtype), (D_QK, F32), (D_VA, F32), (D_QK, BF16), (D_VA, BF16), (2 * d, F32)]
    return pl.pallas_call(
        functools.partial(_inproj_kernel, d_model=d),
        grid=(m // tm,),
        in_specs=[row(d), _const_spec((1, d)), _const_spec((d, d_in))],
        out_specs=[row(n) for n, _ in outs],
        out_shape=[jax.ShapeDtypeStruct((m, n), dt) for n, dt in outs],
        compiler_params=_cparams(("parallel",)),
        name="inproj",
    )(x2, g.reshape(1, d), w_bf)


def _prep_kernel(p_ref, prev_ref, mu_ref, w0_ref, ww2_ref, a0_ref, wa2_ref, kk_ref, ka_ref, rk_ref, sel_ref,
                 r_out, w_out, k_out, v_out, kk_out, kka_out, bonus_out, *, shift_rows):
    p = p_ref[0]
    if shift_rows:
        first = pl.program_id(1) == 0
        before = jnp.where(first, 0.0, prev_ref[0][7:8])
        rolled = pltpu.roll(p, 1, axis=0)
        rows = lax.broadcasted_iota(jnp.int32, p.shape, 0)
        prev = jnp.where(rows == 0, before, rolled)
    else:
        prev = prev_ref[0]
    m = p + (prev - p) * mu_ref[...]
    r = m[:, :D_RWKV]
    k = m[:, D_RWKV:2 * D_RWKV]
    v = m[:, 2 * D_RWKV:3 * D_RWKV]
    w_lo = m[:, 3 * D_RWKV:3 * D_RWKV + LORA]
    a_lo = m[:, 3 * D_RWKV + LORA:]
    sel = sel_ref[...]
    w = jnp.exp(-DECAY_SCALE * jax.nn.sigmoid(
        w0_ref[...] + jnp.dot(jnp.tanh(w_lo), ww2_ref[...], precision=HI, preferred_element_type=F32)))
    a = jax.nn.sigmoid(a0_ref[...] + jnp.dot(a_lo, wa2_ref[...], precision=HI, preferred_element_type=F32))
    kk = k * kk_ref[...]
    k2 = k * (1.0 + (a - 1.0) * ka_ref[...])
    ss = _seg_sum2(kk * kk, sel)
    kkn = kk / jnp.maximum(jnp.sqrt(ss), 1e-12)
    r_out[0] = r
    w_out[0] = w
    k_out[0] = k2
    v_out[0] = v
    kk_out[0] = kkn
    kka_out[0] = kkn * a
    bonus_out[0] = _seg_sum2(r * k2 * rk_ref[...], sel) * v


def _prep(p3, prev3, lw, sel512):
    b, tp, c = p3.shape
    shift_rows = prev3 is None
    tt = _pick(tp, (384, 256, 128))
    cur = pl.BlockSpec((1, tt, c), lambda i, j: (i, j, 0))
    if shift_rows:
        nb8 = tt // 8
        prev_spec = pl.BlockSpec((1, 8, c), lambda i, j: (i, jnp.maximum(j * nb8 - 1, 0), 0))
        prev_arr = p3
    else:
        prev_spec = cur
        prev_arr = prev3
    out = pl.BlockSpec((1, tt, D_RWKV), lambda i, j: (i, j, 0))
    vec = lambda n: _const_spec((1, n))
    return pl.pallas_call(
        functools.partial(_prep_kernel, shift_rows=shift_rows),
        grid=(b, tp // tt),
        in_specs=[cur, prev_spec, vec(c), vec(D_RWKV), _const_spec((LORA, D_RWKV)), vec(D_RWKV),
                  _const_spec((LORA, D_RWKV)), vec(D_RWKV), vec(D_RWKV), vec(D_RWKV),
                  _const_spec((D_RWKV, D_RWKV))],
        out_specs=[out] * 7,
        out_shape=[jax.ShapeDtypeStruct((b, tp, D_RWKV), F32)] * 7,
        compiler_params=_cparams(("parallel", "parallel")),
        name="prep",
    )(p3, prev_arr, lw['shift_mu'], lw['w0'], lw['w_w2'], lw['a0'], lw['w_a2'], lw['k_k'], lw['k_a'],
      lw['r_k'], sel512)


WKV_W = 4 * N_RWKV


def _wkv_kernel(r_ref, w_ref, k_ref, v_ref, kk_ref, kka_ref, s0_ref, y_ref, sT_ref, s_scr, ybuf, vb_scr, *, t_valid,
                nb, tc, gsz):
    c = pl.program_id(1)
    nq = D_RWKV // WKV_W

    @pl.when(c == 0)
    def _():
        s_scr[...] = s0_ref[...].reshape(s_scr.shape)

    nsteps = jnp.clip(t_valid - c * tc, 0, tc)

    @pl.when(nsteps < tc)
    def _():
        y_ref[...] = jnp.zeros_like(y_ref)

    sel = _seg_ones(WKV_W, N_RWKV).astype(BF16)
    eye = jnp.where(lax.broadcasted_iota(jnp.int32, (N_RWKV, WKV_W), 0)
                    == lax.broadcasted_iota(jnp.int32, (N_RWKV, WKV_W), 1) % N_RWKV, 1.0, 0.0).astype(F32)[None]
    eye_b = eye.astype(BF16)
    lane8 = lax.broadcasted_iota(jnp.int32, (8, WKV_W), 1)
    lane_blk = (lane8 % N_RWKV) // 8
    eye8 = jnp.where(lax.broadcasted_iota(jnp.int32, (8, WKV_W), 0) == lane8 % 8, 1.0, 0.0).astype(F32)
    chains = [(b, q) for b in range(nb) for q in range(nq)]

    nch = len(chains)
    tile = (nch, N_RWKV, WKV_W)

    def seg_sums(*xs):
        lhs = jnp.concatenate([x.astype(BF16).reshape(-1, WKV_W) for x in xs], axis=0)
        o = jnp.dot(lhs, sel, preferred_element_type=F32)
        n = nch * N_RWKV
        return [o[j * n:(j + 1) * n].reshape(tile) for j in range(len(xs))]

    def group(g, carry):
        t0 = pl.multiple_of(g * gsz, gsz)

        def rows(ref, i):
            return jnp.stack([ref[b, pl.ds(t0, gsz), WKV_W * q:WKV_W * (q + 1)][i:i + 1] for b, q in chains])

        vbs = seg_sums(*[rows(v_ref, i).astype(BF16) * eye_b for i in range(gsz)])
        for i in range(gsz):
            vb_scr[i] = vbs[i]
        s = s_scr[...]
        sk, = seg_sums(s * rows(kk_ref, 0))
        for i in range(gsz):
            s = s * rows(w_ref, i) - sk * rows(kka_ref, i) + vb_scr[i] * rows(k_ref, i)
            if i + 1 < gsz:
                sk, yb = seg_sums(s * rows(kk_ref, i + 1), s * rows(r_ref, i))
            else:
                yb, = seg_sums(s * rows(r_ref, i))
            yb4 = yb.reshape(nch, N_RWKV // 8, 8, WKV_W)
            diag = yb4[:, 0]
            for blk in range(1, N_RWKV // 8):
                diag = jnp.where(lane_blk == blk, yb4[:, blk], diag)
            ybuf[:, i:i + 1, :] = jnp.sum(diag * eye8, axis=1, keepdims=True)
        s_scr[...] = s
        for n, (b, q) in enumerate(chains):
            y_ref[b, pl.ds(t0, gsz), WKV_W * q:WKV_W * (q + 1)] = ybuf[n]
        return carry

    lax.fori_loop(0, nsteps // gsz, group, 0)

    @pl.when(c == pl.num_programs(1) - 1)
    def _():
        sT_ref[...] = s_scr[...].reshape(sT_ref.shape)


WKV_G = 8
WKV_UNROLL = 2


def _wkv_group_kernel(r_ref, w_ref, k_ref, v_ref, kk_ref, kka_ref, s0_ref, y_ref, sT_ref, s_scr, *, t_valid, nb, tc):
    c = pl.program_id(1)
    nq = D_RWKV // WKV_W
    g_sz = WKV_G
    chains = [(b, q) for b in range(nb) for q in range(nq)]
    nch = len(chains)
    heads = WKV_W // N_RWKV

    @pl.when(c == 0)
    def _():
        s_scr[...] = s0_ref[...].reshape(s_scr.shape)

    nsteps = jnp.clip(t_valid - c * tc, 0, tc)

    @pl.when(nsteps < tc)
    def _():
        y_ref[...] = jnp.zeros_like(y_ref)

    sel = _seg_ones(WKV_W, N_RWKV).astype(BF16)
    row8 = lax.broadcasted_iota(jnp.int32, (g_sz, WKV_W), 0)
    head16 = lax.broadcasted_iota(jnp.int32, (2 * g_sz, WKV_W), 1) // N_RWKV
    head64 = lax.broadcasted_iota(jnp.int32, (N_RWKV, WKV_W), 1) // N_RWKV

    def by_step(tiles, i):
        return jnp.concatenate([t[i:i + 1] for t in tiles], axis=0)

    def prelude(t0):
        tile = lambda ref, b, q: ref[b, pl.ds(t0, g_sz), WKV_W * q:WKV_W * (q + 1)]

        kap_p, rh_p, kh_p, bh_p, v_p, gend = [], [], [], [], [], []
        for b, q in chains:
            w = tile(w_ref, b, q)
            gam, shift = w, 1
            while shift < g_sz:
                gam = gam * jnp.where(row8 >= shift, pltpu.roll(gam, shift, axis=0), 1.0)
                shift *= 2
            gprev = jnp.where(row8 >= 1, pltpu.roll(gam, 1, axis=0), 1.0)
            ginv = 1.0 / gam
            kap_p.append(tile(kk_ref, b, q) * gprev)
            rh_p.append(tile(r_ref, b, q) * gam)
            kh_p.append(tile(k_ref, b, q) * ginv)
            bh_p.append(tile(kka_ref, b, q) * ginv)
            v_p.append(tile(v_ref, b, q))
            gend.append(gam[g_sz - 1:g_sz])

        kap_s = [by_step(kap_p, i) for i in range(g_sz)]
        rh_s = [by_step(rh_p, i) for i in range(g_sz)]
        kh_s = [by_step(kh_p, i) for i in range(g_sz)]
        bh_s = [by_step(bh_p, i) for i in range(g_sz)]
        v_s = [by_step(v_p, i) for i in range(g_sz)]

        prods, index = [], {}
        for t in range(g_sz):
            for s in range(t + 1):
                if s < t:
                    index['a', t, s] = len(prods); prods.append(kap_s[t] * kh_s[s])
                    index['b', t, s] = len(prods); prods.append(kap_s[t] * bh_s[s])
                index['c', t, s] = len(prods); prods.append(rh_s[t] * kh_s[s])
                index['d', t, s] = len(prods); prods.append(rh_s[t] * bh_s[s])
        dots = jnp.dot(jnp.concatenate(prods, axis=0).astype(BF16), sel, preferred_element_type=F32)
        coef = lambda kind, t, s: dots[index[kind, t, s] * nch:(index[kind, t, s] + 1) * nch]
        return kap_p, rh_p, kh_p, bh_p, v_p, gend, v_s, coef

    def chain(t0, pre):
        kap_p, rh_p, kh_p, bh_p, v_p, gend, v_s, coef = pre

        u0_p, y0_p = [], []
        for n in range(nch):
            sb = s_scr[n].astype(BF16)
            lhs = jnp.concatenate([kap_p[n], rh_p[n]], axis=0)
            lhs4 = jnp.concatenate([jnp.where(head16 == h, lhs, 0.0) for h in range(heads)], axis=0).astype(BF16)
            out = lax.dot_general(lhs4, jnp.concatenate([sb] * heads, axis=0), (((1,), (1,)), ((), ())),
                                  preferred_element_type=F32)
            uy = out[:2 * g_sz]
            for h in range(1, heads):
                uy = jnp.where(head16 == h, out[2 * g_sz * h:2 * g_sz * (h + 1)], uy)
            u0_p.append(uy[:g_sz])
            y0_p.append(uy[g_sz:])

        us, ys = [], []
        for t in range(g_sz):
            u = by_step(u0_p, t)
            y = by_step(y0_p, t)
            for s in range(t):
                u = u + coef('a', t, s) * v_s[s] - coef('b', t, s) * us[s]
            us.append(u)
            for s in range(t + 1):
                y = y + coef('c', t, s) * v_s[s] - coef('d', t, s) * us[s]
            ys.append(y)

        for n, (b, q) in enumerate(chains):
            u_n = by_step(us, n)
            x = jnp.concatenate([v_p[n], -u_n], axis=0).astype(BF16)
            z = jnp.concatenate([kh_p[n], bh_p[n]], axis=0).astype(BF16)
            full = lax.dot_general(x, z, (((0,), (0,)), ((), ())), preferred_element_type=F32)
            ds = full[:N_RWKV]
            for h in range(1, heads):
                ds = jnp.where(head64 == h, full[N_RWKV * h:N_RWKV * (h + 1)], ds)
            s_scr[n] = (s_scr[n] + ds) * gend[n]
            y_ref[b, pl.ds(t0, g_sz), WKV_W * q:WKV_W * (q + 1)] = by_step(ys, n)

    def trip(i, carry):
        for sub in range(WKV_UNROLL):
            t0 = pl.multiple_of(i * (WKV_UNROLL * g_sz) + sub * g_sz, g_sz)
            chain(t0, prelude(t0))
        return carry

    lax.fori_loop(0, nsteps // (WKV_UNROLL * g_sz), trip, 0)

    @pl.when(c == pl.num_programs(1) - 1)
    def _():
        sT_ref[...] = s_scr[...].reshape(sT_ref.shape)


def _wkv(r, w, k, v, kk, kka, s0, t_valid):
    b, tp, _ = r.shape
    nq = D_RWKV // WKV_W
    nb = _pick(b, (4, 2, 1))
    tc = _pick(tp, (128,))
    grouped = tc % WKV_G == 0
    gsz = WKV_G if grouped else tc
    trip = WKV_G * WKV_UNROLL if grouped else tc
    assert t_valid % trip == 0 and tc % trip == 0, "sequence length must fill whole loop trips"
    seq = pl.BlockSpec((nb, tc, D_RWKV), lambda i, c: (i, c, 0))
    st = pl.BlockSpec((nb, nq, N_RWKV, WKV_W), lambda i, c: (i, 0, 0, 0))
    state = pltpu.VMEM((nb * nq, N_RWKV, WKV_W), F32)
    if grouped:
        body = functools.partial(_wkv_group_kernel, t_valid=t_valid, nb=nb, tc=tc)
        scratch = [state]
    else:
        body = functools.partial(_wkv_kernel, t_valid=t_valid, nb=nb, tc=tc, gsz=gsz)
        scratch = [state, pltpu.VMEM((nb * nq, gsz, WKV_W), F32), pltpu.VMEM((gsz, nb * nq, N_RWKV, WKV_W), F32)]
    return pl.pallas_call(
        body,
        grid=(b // nb, tp // tc),
        in_specs=[seq] * 6 + [st],
        out_specs=[seq, st],
        out_shape=[jax.ShapeDtypeStruct((b, tp, D_RWKV), F32),
                   jax.ShapeDtypeStruct((b, nq, N_RWKV, WKV_W), F32)],
        scratch_shapes=scratch,
        compiler_params=_cparams(("parallel", "arbitrary")),
        name="wkv",
    )(r, w, k, v, kk, kka, s0)


def _to_groups(s):
    b = s.shape[0]
    nq = D_RWKV // WKV_W
    return s.reshape(b, nq, H_RWKV // nq, N_RWKV, N_RWKV).transpose(0, 1, 3, 2, 4).reshape(b, nq, N_RWKV, WKV_W)


def _from_groups(s):
    b = s.shape[0]
    nq = D_RWKV // WKV_W
    return s.reshape(b, nq, N_RWKV, H_RWKV // nq, N_RWKV).transpose(0, 1, 3, 2, 4).reshape(
        b, H_RWKV, N_RWKV, N_RWKV)


def _lam(lq1, lk1, lq2, lk2, lam_init):
    return (jnp.exp(jnp.sum(lq1 * lk1, axis=-1, keepdims=True))
            - jnp.exp(jnp.sum(lq2 * lk2, axis=-1, keepdims=True)) + lam_init)


def _subln(o, g, lam_init):
    return o * lax.rsqrt(jnp.mean(o * o, axis=-1, keepdims=True) + RMS_EPS) * g * (1.0 - lam_init)


def _flash_kernel(q_ref, k_ref, v_ref, lq1, lk1, lq2, lk2, g_ref, o_ref, q2_scr, m_scr, l_scr, acc_scr, *, lam_init,
                  tq):
    qi = pl.program_id(1)
    m_scr[...] = jnp.full_like(m_scr, NEG)
    l_scr[...] = jnp.zeros_like(l_scr)
    acc_scr[...] = jnp.zeros_like(acc_scr)
    lane = lax.broadcasted_iota(jnp.int32, (tq, LANES), 1)
    for h in range(H_ATT):
        qh = q_ref[0, :, LANES * h:LANES * (h + 1)]
        q2_scr[h, :tq] = jnp.where(lane < DH_ATT, qh, jnp.zeros_like(qh))
        q2_scr[h, tq:] = jnp.where(lane >= DH_ATT, qh, jnp.zeros_like(qh))

    def block(k0, masked):
        if masked:
            tri = lax.broadcasted_iota(jnp.int32, (tq, tq), 1) <= lax.broadcasted_iota(jnp.int32, (tq, tq), 0)
            keep = jnp.concatenate([tri, tri], axis=0)
        ones = jnp.ones((tq, LANES), BF16)
        for h in range(H_ATT):
            ln = slice(LANES * h, LANES * (h + 1))
            kh = k_ref[0, pl.ds(k0, tq), ln]
            v1 = jnp.concatenate([v_ref[0, pl.ds(k0, tq), ln], ones], axis=1)
            s = lax.dot_general(q2_scr[h], kh, (((1,), (1,)), ((), ())), preferred_element_type=F32)
            if masked:
                s = jnp.where(keep, s, NEG)
            m_prev = m_scr[h]
            m_new = jnp.maximum(m_prev, jnp.max(s, axis=-1, keepdims=True))
            alpha = jnp.exp(m_prev - m_new)
            p = jnp.exp((s - jnp.concatenate([m_new] * (tq // LANES), axis=1)).astype(BF16))
            pv = jnp.dot(p, v1, preferred_element_type=F32)
            acc_scr[h] = alpha * acc_scr[h] + pv[:, :LANES]
            l_scr[h] = alpha * l_scr[h] + pv[:, LANES:]
            m_scr[h] = m_new

    def body(ki, carry):
        block(pl.multiple_of(ki * tq, tq), False)
        return carry

    lax.fori_loop(0, qi, body, 0)
    block(pl.multiple_of(qi * tq, tq), True)
    lam = _lam(lq1[...], lk1[...], lq2[...], lk2[...], lam_init)
    for h in range(H_ATT):
        o = acc_scr[h] / l_scr[h]
        o_ref[0, :, LANES * h:LANES * (h + 1)] = _subln(o[:tq] - lam * o[tq:], g_ref[...], lam_init)


def _flash(qb, kb, vb, lam_rows, g, lam_init):
    b, tp, _ = qb.shape
    tq = _pick(tp, (384, 256, 128))
    qspec = pl.BlockSpec((1, tq, D_QK), lambda i, q: (i, q, 0))
    kspec = pl.BlockSpec((1, tp, D_QK), lambda i, q: (i, 0, 0))
    vec = _const_spec((1, DH_ATT))
    return pl.pallas_call(
        functools.partial(_flash_kernel, lam_init=lam_init, tq=tq),
        grid=(b, tp // tq),
        in_specs=[qspec, kspec, kspec, vec, vec, vec, vec, _const_spec((1, DV_ATT))],
        out_specs=qspec,
        out_shape=jax.ShapeDtypeStruct((b, tp, D_VA), F32),
        scratch_shapes=[pltpu.VMEM((H_ATT, 2 * tq, LANES), BF16)] + [pltpu.VMEM((H_ATT, 2 * tq, LANES), F32)] * 3,
        compiler_params=_cparams(("parallel", "arbitrary")),
        name="flash",
    )(qb, kb, vb, *lam_rows, g)


PAGED_ROWS = 16


def _paged_kernel(pt_ref, q_ref, kn_ref, vn_ref, *rest, n_pages, lam_init):
    k_refs, v_refs = rest[:n_pages], rest[n_pages:2 * n_pages]
    lq1, lk1, lq2, lk2, g_ref, o_ref = rest[2 * n_pages:]
    page = v_refs[0].shape[0] // H_ATT
    rows = lax.broadcasted_iota(jnp.int32, (PAGED_ROWS, D_QK), 0)
    cols = lax.broadcasted_iota(jnp.int32, (PAGED_ROWS, D_QK), 1) // DH_ATT
    q8 = jnp.where(rows == cols, q_ref[0], 0.0)
    q_hi = q8.astype(BF16)
    q2 = jnp.concatenate([q_hi, (q8 - q_hi.astype(F32)).astype(BF16)], axis=0)
    scores = []
    for i in range(n_pages):
        kt = k_refs[i][...].reshape(D_QK, page).astype(BF16)
        s2 = jnp.dot(q2, kt, preferred_element_type=F32)
        scores.append(s2[:PAGED_ROWS] + s2[PAGED_ROWS:])
    s_self = jnp.sum(q8 * kn_ref[0], axis=-1, keepdims=True)
    m = s_self
    for s in scores:
        m = jnp.maximum(m, jnp.max(s, axis=-1, keepdims=True))
    p_self = jnp.exp(s_self - m)
    l = p_self
    acc = [p_self * vn_ref[0][:, LANES * h:LANES * (h + 1)] for h in range(H_ATT)]
    for i in range(n_pages):
        p = jnp.exp(scores[i] - m)
        l = l + jnp.sum(p, axis=-1, keepdims=True)
        pb = p.astype(BF16)
        for h in range(H_ATT):
            vh = v_refs[i][pl.ds(h, page, stride=H_ATT), :].astype(BF16)
            acc[h] = acc[h] + jnp.dot(pb, vh, preferred_element_type=F32)
    lam = _lam(lq1[...], lk1[...], lq2[...], lk2[...], lam_init)
    for h in range(H_ATT):
        o = acc[h] / l
        d = o[2 * h:2 * h + 1] - lam * o[2 * h + 1:2 * h + 2]
        o_ref[0, :, LANES * h:LANES * (h + 1)] = _subln(d, g_ref[...], lam_init)


def _paged(q, kn, vn, cache_kt, cache_v2, layer, page_table, lam_rows, g, lam_init):
    db = q.shape[0]
    n_pages = page_table.shape[1]
    page = cache_kt.shape[-1]
    tok = pl.BlockSpec((1, 1, D_QK), lambda b, pt: (b, 0, 0))
    kpg = lambda i: pl.BlockSpec((None, None, 2 * H_ATT, DH_ATT, page),
                                 lambda b, pt: (layer, pt[b * n_pages + i], 0, 0, 0))
    vpg = lambda i: pl.BlockSpec((None, None, page * H_ATT, DV_ATT), lambda b, pt: (layer, pt[b * n_pages + i], 0, 0))
    vec = pl.BlockSpec((1, DH_ATT), lambda b, pt: (0, 0))
    grid_spec = pltpu.PrefetchScalarGridSpec(
        num_scalar_prefetch=1,
        grid=(db,),
        in_specs=([tok, tok, tok] + [kpg(i) for i in range(n_pages)] + [vpg(i) for i in range(n_pages)]
                  + [vec, vec, vec, vec, pl.BlockSpec((1, DV_ATT), lambda b, pt: (0, 0))]),
        out_specs=tok,
    )
    return pl.pallas_call(
        functools.partial(_paged_kernel, n_pages=n_pages, lam_init=lam_init),
        grid_spec=grid_spec,
        out_shape=jax.ShapeDtypeStruct((db, 1, D_VA), F32),
        compiler_params=_cparams(("parallel",)),
        name="paged",
    )(page_table.reshape(-1), q, kn, vn, *([cache_kt] * n_pages), *([cache_v2] * n_pages), *lam_rows, g)


def _mix_kernel(x_ref, y_ref, bonus_ref, ob_ref, gate_ref, lng_ref, lnb_ref, sel_ref, wpa_ref, wpb_ref, wo_ref,
                nm_ref, wup_ref, wdn_ref, nf_ref, xo_ref, *, d_model, ff_chunk, final):
    sel = sel_ref[...]
    y = y_ref[...]
    mu = _seg_sum2(y, sel) * (1.0 / N_RWKV)
    d = y - mu
    var = _seg_sum2(d * d, sel) * (1.0 / N_RWKV)
    o_a = d * lax.rsqrt(var + LNX_EPS) * lng_ref[...] + lnb_ref[...] + bonus_ref[...]
    gates = gate_ref[...]
    merged = (gates[:, :d_model] * jnp.dot(o_a.astype(BF16), wpa_ref[...], preferred_element_type=F32)
              + gates[:, d_model:] * jnp.dot(ob_ref[...].astype(BF16), wpb_ref[...], preferred_element_type=F32))
    x = x_ref[...] + jnp.dot(merged.astype(BF16), wo_ref[...], preferred_element_type=F32)
    h2 = (x * lax.rsqrt(jnp.mean(x * x, axis=-1, keepdims=True) + RMS_EPS) * nm_ref[...]).astype(BF16)
    d_ff = wup_ref.shape[1]
    for c in range(d_ff // ff_chunk):
        cs = slice(c * ff_chunk, (c + 1) * ff_chunk)
        u = jnp.maximum(jnp.dot(h2, wup_ref[:, cs], preferred_element_type=F32), 0.0)
        x = x + jnp.dot((u * u).astype(BF16), wdn_ref[cs, :], preferred_element_type=F32)
    if final:
        x = x * lax.rsqrt(jnp.mean(x * x, axis=-1, keepdims=True) + RMS_EPS) * nf_ref[...]
    xo_ref[...] = x


def _mix(x2, y2, bonus2, ob2, gates2, lw, sel512, norm_final, final):
    m, d = x2.shape
    d_ff = lw['w_up_b'].shape[1]
    tm = _pick(m, (256, 128))
    row = lambda n: pl.BlockSpec((tm, n), lambda i: (i, 0))
    cs = _const_spec
    return pl.pallas_call(
        functools.partial(_mix_kernel, d_model=d, ff_chunk=min(d_ff, 1024), final=final),
        grid=(m // tm,),
        in_specs=[row(d), row(D_RWKV), row(D_RWKV), row(D_VA), row(2 * d), cs((1, D_RWKV)), cs((1, D_RWKV)),
                  cs((D_RWKV, D_RWKV)), cs((D_RWKV, d)), cs((D_VA, d)), cs((d, d)), cs((1, d)),
                  cs((d, d_ff)), cs((d_ff, d)), cs((1, d))],
        out_specs=row(d),
        out_shape=jax.ShapeDtypeStruct((m, d), F32),
        compiler_params=_cparams(("parallel",)),
        name="mix",
    )(x2, y2, bonus2, ob2, gates2, lw['lnx_g'], lw['lnx_b'], sel512, lw['w_pa_b'], lw['w_pb_b'], lw['w_o_b'],
      lw['norm_mlp'], lw['w_up_b'], lw['w_down_b'], norm_final)


def _layer(x3, lw, sel512, lam_init, t_valid, prev3, s0_pairs, attend, norm_final, final):
    b, tp, d = x3.shape
    m = b * tp
    pr, q, k, v, kb, vb, gates = _inproj(x3.reshape(m, d), lw['norm_mix'], lw['w_in_b'],
                                         BF16 if prev3 is None else F32)
    three = lambda a: a.reshape(b, tp, a.shape[-1])
    pr3 = three(pr)
    r, w, k2, vv, kk, kka, bonus = _prep(pr3, prev3, lw, sel512)
    if prev3 is not None:
        seqs = lambda a: a.reshape(m, 1, D_RWKV)
        y, s_fin = _wkv(seqs(r), seqs(w), seqs(k2), seqs(vv), seqs(kk), seqs(kka), s0_pairs, 1)
    else:
        y, s_fin = _wkv(r, w, k2, vv, kk, kka, s0_pairs, t_valid)
    ob = attend(three(q), three(k), three(v), three(kb), three(vb))
    xo = _mix(x3.reshape(m, d), y.reshape(m, D_RWKV), bonus.reshape(m, D_RWKV), ob.reshape(m, D_VA), gates,
              lw, sel512, norm_final, final)
    return xo.reshape(b, tp, d), three(k), three(v), s_fin, pr3


def kernel(x_prompt, x_sample, cache_k, cache_v, state_wkv, state_shift, page_table, meta_tokens, norm_mix, w_in,
           shift_mu, w0, w_w2, a0, w_a2, k_k, k_a, r_k, lnx_g, lnx_b, lambda_q1, lambda_k1, lambda_q2, lambda_k2,
           subln_g, w_pa, w_pb, w_o, norm_mlp, w_up, w_down, norm_final):
    bsz, seq, d = x_prompt.shape
    db = x_sample.shape[0]
    depth = w_in.shape[0]
    n_meta = meta_tokens.shape[0]
    t = seq + n_meta
    tp = -(-t // LANES) * LANES
    n_pool, page = cache_k.shape[1], cache_k.shape[2]

    xp = jnp.concatenate([jnp.broadcast_to(meta_tokens[None].astype(x_prompt.dtype), (bsz, n_meta, d)), x_prompt,
                          jnp.zeros((bsz, tp - t, d), x_prompt.dtype)], axis=1)
    xs = x_sample.reshape(1, db, d)
    cache_kt = cache_k.transpose(0, 1, 3, 4, 2)
    cache_v2 = cache_v.reshape(depth, n_pool, page * H_ATT, DV_ATT)
    sel512 = _seg_ones(D_RWKV, N_RWKV).astype(BF16)
    nf = norm_final.reshape(1, d)
    zero_state = jnp.zeros((bsz, D_RWKV // WKV_W, N_RWKV, WKV_W), F32)

    kp_l, vp_l, wp_l, sp_l, ks_l, vs_l, ws_l, ss_l = ([] for _ in range(8))
    for l in range(depth):
        row = lambda a: a[l].reshape(1, -1)
        lw = {
            'norm_mix': norm_mix[l], 'w_in_b': w_in[l].astype(BF16), 'shift_mu': row(shift_mu), 'w0': row(w0),
            'w_w2': w_w2[l], 'a0': row(a0), 'w_a2': w_a2[l], 'k_k': row(k_k), 'k_a': row(k_a), 'r_k': row(r_k),
            'lnx_g': row(lnx_g), 'lnx_b': row(lnx_b), 'w_pa_b': w_pa[l].astype(BF16),
            'w_pb_b': w_pb[l].astype(BF16), 'w_o_b': w_o[l].astype(BF16), 'norm_mlp': row(norm_mlp),
            'w_up_b': w_up[l].astype(BF16), 'w_down_b': w_down[l].astype(BF16),
        }
        lam_init = 0.8 - 0.6 * math.exp(-0.3 * l)
        lam_rows = (row(lambda_q1), row(lambda_k1), row(lambda_q2), row(lambda_k2))
        g = row(subln_g)
        final = l == depth - 1

        attend_p = lambda q, k, v, kb, vb: _flash(q, kb, vb, lam_rows, g, lam_init)
        xp, k_new, v_new, s_fin, pr3 = _layer(xp, lw, sel512, lam_init, t, None, zero_state, attend_p, nf, final)
        kp_l.append(k_new[:, :t].reshape(bsz, t, 2 * H_ATT, DH_ATT))
        vp_l.append(v_new[:, :t].reshape(bsz, t, H_ATT, DV_ATT))
        wp_l.append(_from_groups(s_fin))
        sp_l.append(pr3[:, t - 1])

        attend_s = lambda q, k, v, kb, vb: _paged(
            q.reshape(db, 1, D_QK), k.reshape(db, 1, D_QK), v.reshape(db, 1, D_VA), cache_kt, cache_v2, l,
            page_table, lam_rows, g, lam_init).reshape(1, db, D_VA)
        xs, k_new, v_new, s_fin, pr3 = _layer(xs, lw, sel512, lam_init, 1, state_shift[l][None],
                                              _to_groups(state_wkv[l]), attend_s, nf, final)
        ks_l.append(k_new.reshape(db, 1, 2 * H_ATT, DH_ATT))
        vs_l.append(v_new.reshape(db, 1, H_ATT, DV_ATT))
        ws_l.append(_from_groups(s_fin).astype(state_wkv.dtype))
        ss_l.append(pr3[0])

    y_prompt = xp[:, n_meta:t]
    y_sample = xs.reshape(db, 1, d)
    return (y_prompt, y_sample, jnp.stack(kp_l), jnp.stack(vp_l), jnp.stack(wp_l), jnp.stack(sp_l),
            jnp.stack(ks_l), jnp.stack(vs_l), jnp.stack(ws_l), jnp.stack(ss_l))
```

```python
import functools
import math

import jax
import jax.numpy as jnp
from jax import lax
from jax.experimental import pallas as pl
from jax.experimental.pallas import tpu as pltpu

F32 = jnp.float32
BF16 = jnp.bfloat16
HI = lax.Precision.HIGHEST

H_RWKV = 8
N_RWKV = 64
D_RWKV = H_RWKV * N_RWKV
LORA = 64
RWKV_COLS = 3 * D_RWKV + 2 * LORA
DECAY_SCALE = 0.606531
LNX_EPS = 64e-5
H_ATT = 4
DH_ATT = 64
DV_ATT = 2 * DH_ATT
D_QK = 2 * H_ATT * DH_ATT
D_VA = H_ATT * DV_ATT
ATT_SCALE = DH_ATT ** -0.5
NEG = -1e30
RMS_EPS = 1e-6

LANES = 128
SEQ_TILES = (384, 256, 128)
VMEM_LIMIT = 56 * 1024 * 1024


def _cparams(sem):
    return pltpu.CompilerParams(dimension_semantics=sem, vmem_limit_bytes=VMEM_LIMIT)


def _pick(n, cands):
    for c in cands:
        if n % c == 0:
            return c
    return n


def _const_spec(shape):
    nd = len(shape)
    return pl.BlockSpec(shape, lambda *_: (0,) * nd, pipeline_mode=pl.Buffered(1))


def _seg_ones(n, seg, scale=1.0):
    i = lax.broadcasted_iota(jnp.int32, (n, n), 0) // seg
    j = lax.broadcasted_iota(jnp.int32, (n, n), 1) // seg
    return jnp.where(i == j, scale, 0.0).astype(F32)


def _seg_sum2(x, sel_b):
    hi = x.astype(BF16)
    lo = (x - hi.astype(F32)).astype(BF16)
    o = jnp.dot(jnp.concatenate([hi, lo], axis=0), sel_b, preferred_element_type=F32)
    return o[:x.shape[0]] + o[x.shape[0]:]


def _inproj_kernel(x_ref, g_ref, w_ref, pr_ref, q_ref, k_ref, v_ref, kb_ref, vb_ref, gate_ref, *, d_model,
                   cache_layout):
    x = x_ref[0]
    hn = x * lax.rsqrt(jnp.mean(x * x, axis=-1, keepdims=True) + RMS_EPS) * g_ref[...]
    hb = hn.astype(BF16)
    tm = x.shape[0]

    def proj(lo, n):
        return jnp.dot(hb, w_ref[:, lo:lo + n], preferred_element_type=F32)

    q0 = RWKV_COLS
    k0 = q0 + D_QK
    v0 = k0 + D_QK
    g0 = v0 + D_VA
    pr_ref[0] = proj(0, RWKV_COLS)
    q_ref[0] = (proj(q0, D_QK) * ATT_SCALE).astype(q_ref.dtype)
    k = proj(k0, D_QK)
    kb_ref[0] = k.astype(BF16)
    v = proj(v0, D_VA)
    vb_ref[0] = v.astype(BF16)
    if cache_layout:
        k_ref[0] = k.T
        for h in range(H_ATT):
            v_ref[0, pl.ds(h, tm, stride=H_ATT), :] = v[:, DV_ATT * h:DV_ATT * (h + 1)]
    else:
        k_ref[0] = k
        v_ref[0] = v
    gate_ref[0] = jax.nn.sigmoid(proj(g0, 2 * d_model))


def _inproj(x3, g, w_bf, q_dtype, cache_layout):
    b, tp, d = x3.shape
    d_in = w_bf.shape[1]
    tm = _pick(tp, SEQ_TILES)
    row = lambda n: pl.BlockSpec((1, tm, n), lambda i, j: (i, j, 0))
    outs = [(RWKV_COLS, F32), (D_QK, q_dtype), (D_QK, F32), (D_VA, F32), (D_QK, BF16), (D_VA, BF16), (2 * d, F32)]
    specs = [row(n) for n, _ in outs]
    shapes = [jax.ShapeDtypeStruct((b, tp, n), dt) for n, dt in outs]
    if cache_layout:
        specs[2] = pl.BlockSpec((1, D_QK, tm), lambda i, j: (i, 0, j))
        shapes[2] = jax.ShapeDtypeStruct((b, D_QK, tp), F32)
        specs[3] = pl.BlockSpec((1, tm * H_ATT, DV_ATT), lambda i, j: (i, j, 0))
        shapes[3] = jax.ShapeDtypeStruct((b, tp * H_ATT, DV_ATT), F32)
    return pl.pallas_call(
        functools.partial(_inproj_kernel, d_model=d, cache_layout=cache_layout),
        grid=(b, tp // tm),
        in_specs=[row(d), _const_spec((1, d)), _const_spec((d, d_in))],
        out_specs=specs,
        out_shape=shapes,
        compiler_params=_cparams(("parallel", "parallel")),
        name="inproj",
    )(x3, g.reshape(1, d), w_bf)


def _prep_kernel(p_ref, prev_ref, mu_ref, w0_ref, ww2_ref, a0_ref, wa2_ref, kk_ref, ka_ref, rk_ref, sel_ref,
                 r_out, w_out, k_out, v_out, kk_out, kka_out, bonus_out, *, shift_rows):
    p = p_ref[0]
    if shift_rows:
        first = pl.program_id(1) == 0
        before = jnp.where(first, 0.0, prev_ref[0][7:8])
        rolled = pltpu.roll(p, 1, axis=0)
        rows = lax.broadcasted_iota(jnp.int32, p.shape, 0)
        prev = jnp.where(rows == 0, before, rolled)
    else:
        prev = prev_ref[0]
    m = p + (prev - p) * mu_ref[...]
    r = m[:, :D_RWKV]
    k = m[:, D_RWKV:2 * D_RWKV]
    v = m[:, 2 * D_RWKV:3 * D_RWKV]
    w_lo = m[:, 3 * D_RWKV:3 * D_RWKV + LORA]
    a_lo = m[:, 3 * D_RWKV + LORA:]
    sel = sel_ref[...]
    w = jnp.exp(-DECAY_SCALE * jax.nn.sigmoid(
        w0_ref[...] + jnp.dot(jnp.tanh(w_lo), ww2_ref[...], precision=HI, preferred_element_type=F32)))
    a = jax.nn.sigmoid(a0_ref[...] + jnp.dot(a_lo, wa2_ref[...], precision=HI, preferred_element_type=F32))
    kk = k * kk_ref[...]
    k2 = k * (1.0 + (a - 1.0) * ka_ref[...])
    ss = _seg_sum2(kk * kk, sel)
    kkn = kk / jnp.maximum(jnp.sqrt(ss), 1e-12)
    r_out[0] = r
    w_out[0] = w
    k_out[0] = k2
    v_out[0] = v
    kk_out[0] = kkn
    kka_out[0] = kkn * a
    bonus_out[0] = _seg_sum2(r * k2 * rk_ref[...], sel) * v


def _prep(p3, prev3, lw, sel512):
    b, tp, c = p3.shape
    shift_rows = prev3 is None
    tt = _pick(tp, SEQ_TILES)
    cur = pl.BlockSpec((1, tt, c), lambda i, j: (i, j, 0))
    if shift_rows:
        nb8 = tt // 8
        prev_spec = pl.BlockSpec((1, 8, c), lambda i, j: (i, jnp.maximum(j * nb8 - 1, 0), 0))
        prev_arr = p3
    else:
        prev_spec = cur
        prev_arr = prev3
    out = pl.BlockSpec((1, tt, D_RWKV), lambda i, j: (i, j, 0))
    vec = lambda n: _const_spec((1, n))
    return pl.pallas_call(
        functools.partial(_prep_kernel, shift_rows=shift_rows),
        grid=(b, tp // tt),
        in_specs=[cur, prev_spec, vec(c), vec(D_RWKV), _const_spec((LORA, D_RWKV)), vec(D_RWKV),
                  _const_spec((LORA, D_RWKV)), vec(D_RWKV), vec(D_RWKV), vec(D_RWKV),
                  _const_spec((D_RWKV, D_RWKV))],
        out_specs=[out] * 7,
        out_shape=[jax.ShapeDtypeStruct((b, tp, D_RWKV), F32)] * 7,
        compiler_params=_cparams(("parallel", "parallel")),
        name="prep",
    )(p3, prev_arr, lw['shift_mu'], lw['w0'], lw['w_w2'], lw['a0'], lw['w_a2'], lw['k_k'], lw['k_a'],
      lw['r_k'], sel512)


WKV_W = 4 * N_RWKV


def _wkv_kernel(r_ref, w_ref, k_ref, v_ref, kk_ref, kka_ref, s0_ref, y_ref, sT_ref, s_scr, ybuf, vb_scr, *, t_valid,
                nb, tc, gsz):
    c = pl.program_id(1)
    nq = D_RWKV // WKV_W

    @pl.when(c == 0)
    def _():
        s_scr[...] = s0_ref[...].reshape(s_scr.shape)

    nsteps = jnp.clip(t_valid - c * tc, 0, tc)

    @pl.when(nsteps < tc)
    def _():
        y_ref[...] = jnp.zeros_like(y_ref)

    sel = _seg_ones(WKV_W, N_RWKV).astype(BF16)
    eye = jnp.where(lax.broadcasted_iota(jnp.int32, (N_RWKV, WKV_W), 0)
                    == lax.broadcasted_iota(jnp.int32, (N_RWKV, WKV_W), 1) % N_RWKV, 1.0, 0.0).astype(F32)[None]
    eye_b = eye.astype(BF16)
    lane8 = lax.broadcasted_iota(jnp.int32, (8, WKV_W), 1)
    lane_blk = (lane8 % N_RWKV) // 8
    eye8 = jnp.where(lax.broadcasted_iota(jnp.int32, (8, WKV_W), 0) == lane8 % 8, 1.0, 0.0).astype(F32)
    chains = [(b, q) for b in range(nb) for q in range(nq)]

    nch = len(chains)
    tile = (nch, N_RWKV, WKV_W)

    def seg_sums(*xs):
        lhs = jnp.concatenate([x.astype(BF16).reshape(-1, WKV_W) for x in xs], axis=0)
        o = jnp.dot(lhs, sel, preferred_element_type=F32)
        n = nch * N_RWKV
        return [o[j * n:(j + 1) * n].reshape(tile) for j in range(len(xs))]

    def group(g, carry):
        t0 = pl.multiple_of(g * gsz, gsz)

        def rows(ref, i):
            return jnp.stack([ref[b, pl.ds(t0, gsz), WKV_W * q:WKV_W * (q + 1)][i:i + 1] for b, q in chains])

        vbs = seg_sums(*[rows(v_ref, i).astype(BF16) * eye_b for i in range(gsz)])
        for i in range(gsz):
            vb_scr[i] = vbs[i]
        s = s_scr[...]
        sk, = seg_sums(s * rows(kk_ref, 0))
        for i in range(gsz):
            s = s * rows(w_ref, i) - sk * rows(kka_ref, i) + vb_scr[i] * rows(k_ref, i)
            if i + 1 < gsz:
                sk, yb = seg_sums(s * rows(kk_ref, i + 1), s * rows(r_ref, i))
            else:
                yb, = seg_sums(s * rows(r_ref, i))
            yb4 = yb.reshape(nch, N_RWKV // 8, 8, WKV_W)
            diag = yb4[:, 0]
            for blk in range(1, N_RWKV // 8):
                diag = jnp.where(lane_blk == blk, yb4[:, blk], diag)
            ybuf[:, i:i + 1, :] = jnp.sum(diag * eye8, axis=1, keepdims=True)
        s_scr[...] = s
        for n, (b, q) in enumerate(chains):
            y_ref[b, pl.ds(t0, gsz), WKV_W * q:WKV_W * (q + 1)] = ybuf[n]
        return carry

    lax.fori_loop(0, nsteps // gsz, group, 0)

    @pl.when(c == pl.num_programs(1) - 1)
    def _():
        sT_ref[...] = s_scr[...].reshape(sT_ref.shape)


WKV_G = 8
WKV_UNROLL = 2


def _wkv_group_kernel(r_ref, w_ref, k_ref, v_ref, kk_ref, kka_ref, s0_ref, y_ref, sT_ref, s_scr, *, t_valid, nb, tc):
    c = pl.program_id(1)
    nq = D_RWKV // WKV_W
    g_sz = WKV_G
    chains = [(b, q) for b in range(nb) for q in range(nq)]
    nch = len(chains)
    heads = WKV_W // N_RWKV

    @pl.when(c == 0)
    def _():
        s_scr[...] = s0_ref[...].reshape(s_scr.shape)

    nsteps = jnp.clip(t_valid - c * tc, 0, tc)

    @pl.when(nsteps < tc)
    def _():
        y_ref[...] = jnp.zeros_like(y_ref)

    sel = _seg_ones(WKV_W, N_RWKV).astype(BF16)
    row8 = lax.broadcasted_iota(jnp.int32, (g_sz, WKV_W), 0)
    head16 = lax.broadcasted_iota(jnp.int32, (2 * g_sz, WKV_W), 1) // N_RWKV
    head64 = lax.broadcasted_iota(jnp.int32, (N_RWKV, WKV_W), 1) // N_RWKV

    def by_step(tiles, i):
        return jnp.concatenate([t[i:i + 1] for t in tiles], axis=0)

    pairs = [(kind, t, s) for t in range(g_sz) for s in range(t + 1) for kind in 'abcd' if s < t or kind in 'cd']
    pair_index = {p: i for i, p in enumerate(pairs)}

    def prelude(t0):
        tile = lambda ref, b, q: ref[b, pl.ds(t0, g_sz), WKV_W * q:WKV_W * (q + 1)]

        kap_p, rh_p, kh_p, bh_p, v_p, gam_p = [], [], [], [], [], []
        for b, q in chains:
            w = tile(w_ref, b, q)
            gam, shift = w, 1
            while shift < g_sz:
                gam = gam * jnp.where(row8 >= shift, pltpu.roll(gam, shift, axis=0), 1.0)
                shift *= 2
            gprev = jnp.where(row8 >= 1, pltpu.roll(gam, 1, axis=0), 1.0)
            ginv = 1.0 / gam
            kap_p.append(tile(kk_ref, b, q) * gprev)
            rh_p.append(tile(r_ref, b, q) * gam)
            kh_p.append(tile(k_ref, b, q) * ginv)
            bh_p.append(tile(kka_ref, b, q) * ginv)
            v_p.append(tile(v_ref, b, q))
            gam_p.append(gam)

        kap_s = [by_step(kap_p, i) for i in range(g_sz)]
        rh_s = [by_step(rh_p, i) for i in range(g_sz)]
        kh_s = [by_step(kh_p, i) for i in range(g_sz)]
        bh_s = [by_step(bh_p, i) for i in range(g_sz)]
        v_s = [by_step(v_p, i) for i in range(g_sz)]

        prods = [(kap_s[t] if kind in 'ab' else rh_s[t]) * (kh_s[s] if kind in 'ac' else bh_s[s])
                 for kind, t, s in pairs]
        dots = jnp.dot(jnp.concatenate(prods, axis=0).astype(BF16), sel, preferred_element_type=F32)
        coef = lambda kind, t, s: dots[pair_index[kind, t, s] * nch:(pair_index[kind, t, s] + 1) * nch]
        return kap_p, rh_p, kh_p, bh_p, v_p, gam_p, v_s, coef

    def chain(t0, pre):
        kap_p, rh_p, kh_p, bh_p, v_p, gam_p, v_s, coef = pre

        u0_p, y0_p = [], []
        for n in range(nch):
            sb = s_scr[n].astype(BF16)
            lhs = jnp.concatenate([kap_p[n], rh_p[n]], axis=0)
            lhs4 = jnp.concatenate([jnp.where(head16 == h, lhs, 0.0) for h in range(heads)], axis=0).astype(BF16)
            out = lax.dot_general(lhs4, jnp.concatenate([sb] * heads, axis=0), (((1,), (1,)), ((), ())),
                                  preferred_element_type=F32)
            uy = out[:2 * g_sz]
            for h in range(1, heads):
                uy = jnp.where(head16 == h, out[2 * g_sz * h:2 * g_sz * (h + 1)], uy)
            u0_p.append(uy[:g_sz])
            y0_p.append(uy[g_sz:])

        us, ys = [], []
        for t in range(g_sz):
            u = by_step(u0_p, t)
            y = by_step(y0_p, t)
            for s in range(t):
                u = u + coef('a', t, s) * v_s[s] - coef('b', t, s) * us[s]
            us.append(u)
            for s in range(t + 1):
                y = y + coef('c', t, s) * v_s[s] - coef('d', t, s) * us[s]
            ys.append(y)

        for n, (b, q) in enumerate(chains):
            u_n = by_step(us, n)
            x = jnp.concatenate([v_p[n], -u_n], axis=0).astype(BF16)
            z = jnp.concatenate([kh_p[n], bh_p[n]], axis=0).astype(BF16)
            full = lax.dot_general(x, z, (((0,), (0,)), ((), ())), preferred_element_type=F32)
            ds = full[:N_RWKV]
            for h in range(1, heads):
                ds = jnp.where(head64 == h, full[N_RWKV * h:N_RWKV * (h + 1)], ds)
            s_scr[n] = (s_scr[n] + ds) * gam_p[n][g_sz - 1:g_sz]
            y_ref[b, pl.ds(t0, g_sz), WKV_W * q:WKV_W * (q + 1)] = by_step(ys, n)

    def trip(i, carry):
        for sub in range(WKV_UNROLL):
            t0 = pl.multiple_of(i * (WKV_UNROLL * g_sz) + sub * g_sz, g_sz)
            chain(t0, prelude(t0))
        return carry

    lax.fori_loop(0, nsteps // (WKV_UNROLL * g_sz), trip, 0)

    @pl.when(c == pl.num_programs(1) - 1)
    def _():
        sT_ref[...] = s_scr[...].reshape(sT_ref.shape)


def _wkv(r, w, k, v, kk, kka, s0, t_valid):
    b, tp, _ = r.shape
    nq = D_RWKV // WKV_W
    nb = _pick(b, (4, 2, 1))
    tc = _pick(tp, (128,))
    grouped = tc % WKV_G == 0
    gsz = WKV_G if grouped else tc
    trip = WKV_G * WKV_UNROLL if grouped else tc
    assert t_valid % trip == 0 and tc % trip == 0, "sequence length must fill whole loop trips"
    seq = pl.BlockSpec((nb, tc, D_RWKV), lambda i, c: (i, c, 0))
    st = pl.BlockSpec((nb, nq, N_RWKV, WKV_W), lambda i, c: (i, 0, 0, 0))
    state = pltpu.VMEM((nb * nq, N_RWKV, WKV_W), F32)
    if grouped:
        body = functools.partial(_wkv_group_kernel, t_valid=t_valid, nb=nb, tc=tc)
        scratch = [state]
    else:
        body = functools.partial(_wkv_kernel, t_valid=t_valid, nb=nb, tc=tc, gsz=gsz)
        scratch = [state, pltpu.VMEM((nb * nq, gsz, WKV_W), F32), pltpu.VMEM((gsz, nb * nq, N_RWKV, WKV_W), F32)]
    return pl.pallas_call(
        body,
        grid=(b // nb, tp // tc),
        in_specs=[seq] * 6 + [st],
        out_specs=[seq, st],
        out_shape=[jax.ShapeDtypeStruct((b, tp, D_RWKV), F32),
                   jax.ShapeDtypeStruct((b, nq, N_RWKV, WKV_W), F32)],
        scratch_shapes=scratch,
        compiler_params=_cparams(("parallel", "arbitrary")),
        name="wkv",
    )(r, w, k, v, kk, kka, s0)


def _to_groups(s):
    b = s.shape[0]
    nq = D_RWKV // WKV_W
    return s.reshape(b, nq, H_RWKV // nq, N_RWKV, N_RWKV).transpose(0, 1, 3, 2, 4).reshape(b, nq, N_RWKV, WKV_W)


def _from_groups(s):
    b = s.shape[0]
    nq = D_RWKV // WKV_W
    return s.reshape(b, nq, N_RWKV, H_RWKV // nq, N_RWKV).transpose(0, 1, 3, 2, 4).reshape(
        b, H_RWKV, N_RWKV, N_RWKV)


def _lam(lq1, lk1, lq2, lk2, lam_init):
    return (jnp.exp(jnp.sum(lq1 * lk1, axis=-1, keepdims=True))
            - jnp.exp(jnp.sum(lq2 * lk2, axis=-1, keepdims=True)) + lam_init)


def _subln(o, g, lam_init):
    return o * lax.rsqrt(jnp.mean(o * o, axis=-1, keepdims=True) + RMS_EPS) * g * (1.0 - lam_init)


def _flash_kernel(q_ref, k_ref, v_ref, lq1, lk1, lq2, lk2, g_ref, o_ref, q2_scr, m_scr, l_scr, acc_scr, *, lam_init,
                  tq):
    qi = pl.program_id(1)
    m_scr[...] = jnp.full_like(m_scr, NEG)
    l_scr[...] = jnp.zeros_like(l_scr)
    acc_scr[...] = jnp.zeros_like(acc_scr)
    lane = lax.broadcasted_iota(jnp.int32, (tq, LANES), 1)
    for h in range(H_ATT):
        qh = q_ref[0, :, LANES * h:LANES * (h + 1)]
        q2_scr[h, :tq] = jnp.where(lane < DH_ATT, qh, jnp.zeros_like(qh))
        q2_scr[h, tq:] = jnp.where(lane >= DH_ATT, qh, jnp.zeros_like(qh))

    def block(k0, masked):
        if masked:
            tri = lax.broadcasted_iota(jnp.int32, (tq, tq), 1) <= lax.broadcasted_iota(jnp.int32, (tq, tq), 0)
            keep = jnp.concatenate([tri, tri], axis=0)
        ones = jnp.ones((tq, LANES), BF16)
        for h in range(H_ATT):
            ln = slice(LANES * h, LANES * (h + 1))
            kh = k_ref[0, pl.ds(k0, tq), ln]
            v1 = jnp.concatenate([v_ref[0, pl.ds(k0, tq), ln], ones], axis=1)
            s = lax.dot_general(q2_scr[h], kh, (((1,), (1,)), ((), ())), preferred_element_type=F32)
            if masked:
                s = jnp.where(keep, s, NEG)
            m_prev = m_scr[h]
            m_new = jnp.maximum(m_prev, jnp.max(s, axis=-1, keepdims=True))
            alpha = jnp.exp(m_prev - m_new)
            p = jnp.exp((s - jnp.concatenate([m_new] * (tq // LANES), axis=1)).astype(BF16))
            pv = jnp.dot(p, v1, preferred_element_type=F32)
            acc_scr[h] = alpha * acc_scr[h] + pv[:, :LANES]
            l_scr[h] = alpha * l_scr[h] + pv[:, LANES:]
            m_scr[h] = m_new

    def body(ki, carry):
        block(pl.multiple_of(ki * tq, tq), False)
        return carry

    lax.fori_loop(0, qi, body, 0)
    block(pl.multiple_of(qi * tq, tq), True)
    lam = _lam(lq1[...], lk1[...], lq2[...], lk2[...], lam_init)
    for h in range(H_ATT):
        o = acc_scr[h] / l_scr[h]
        o_ref[0, :, LANES * h:LANES * (h + 1)] = _subln(o[:tq] - lam * o[tq:], g_ref[...], lam_init)


def _flash(qb, kb, vb, lam_rows, g, lam_init):
    b, tp, _ = qb.shape
    tq = _pick(tp, SEQ_TILES)
    qspec = pl.BlockSpec((1, tq, D_QK), lambda i, q: (i, q, 0))
    kspec = pl.BlockSpec((1, tp, D_QK), lambda i, q: (i, 0, 0))
    vec = _const_spec((1, DH_ATT))
    return pl.pallas_call(
        functools.partial(_flash_kernel, lam_init=lam_init, tq=tq),
        grid=(b, tp // tq),
        in_specs=[qspec, kspec, kspec, vec, vec, vec, vec, _const_spec((1, DV_ATT))],
        out_specs=qspec,
        out_shape=jax.ShapeDtypeStruct((b, tp, D_VA), F32),
        scratch_shapes=[pltpu.VMEM((H_ATT, 2 * tq, LANES), BF16)] + [pltpu.VMEM((H_ATT, 2 * tq, LANES), F32)] * 3,
        compiler_params=_cparams(("parallel", "arbitrary")),
        name="flash",
    )(qb, kb, vb, *lam_rows, g)


PAGED_ROWS = 16


def _paged_kernel(pt_ref, q_ref, kn_ref, vn_ref, *rest, n_pages, lam_init):
    k_refs, v_refs = rest[:n_pages], rest[n_pages:2 * n_pages]
    lq1, lk1, lq2, lk2, g_ref, o_ref = rest[2 * n_pages:]
    page = v_refs[0].shape[0] // H_ATT
    rows = lax.broadcasted_iota(jnp.int32, (PAGED_ROWS, D_QK), 0)
    cols = lax.broadcasted_iota(jnp.int32, (PAGED_ROWS, D_QK), 1) // DH_ATT
    q8 = jnp.where(rows == cols, q_ref[0], 0.0)
    q_hi = q8.astype(BF16)
    q2 = jnp.concatenate([q_hi, (q8 - q_hi.astype(F32)).astype(BF16)], axis=0)
    scores = []
    for i in range(n_pages):
        kt = k_refs[i][...].reshape(D_QK, page).astype(BF16)
        s2 = jnp.dot(q2, kt, preferred_element_type=F32)
        scores.append(s2[:PAGED_ROWS] + s2[PAGED_ROWS:])
    s_self = jnp.sum(q8 * kn_ref[0], axis=-1, keepdims=True)
    m = s_self
    for s in scores:
        m = jnp.maximum(m, jnp.max(s, axis=-1, keepdims=True))
    p_self = jnp.exp(s_self - m)
    l = p_self
    acc = [p_self * vn_ref[0][:, LANES * h:LANES * (h + 1)] for h in range(H_ATT)]
    for i in range(n_pages):
        p = jnp.exp(scores[i] - m)
        l = l + jnp.sum(p, axis=-1, keepdims=True)
        pb = p.astype(BF16)
        for h in range(H_ATT):
            vh = v_refs[i][pl.ds(h, page, stride=H_ATT), :].astype(BF16)
            acc[h] = acc[h] + jnp.dot(pb, vh, preferred_element_type=F32)
    lam = _lam(lq1[...], lk1[...], lq2[...], lk2[...], lam_init)
    for h in range(H_ATT):
        o = acc[h] / l
        d = o[2 * h:2 * h + 1] - lam * o[2 * h + 1:2 * h + 2]
        o_ref[0, :, LANES * h:LANES * (h + 1)] = _subln(d, g_ref[...], lam_init)


def _paged(q, kn, vn, cache_kt, cache_v2, layer, page_table, lam_rows, g, lam_init):
    db = q.shape[0]
    n_pages = page_table.shape[1]
    page = cache_kt.shape[-1]
    tok = pl.BlockSpec((1, 1, D_QK), lambda b, pt: (b, 0, 0))
    kpg = lambda i: pl.BlockSpec((None, None, 2 * H_ATT, DH_ATT, page),
                                 lambda b, pt: (layer, pt[b * n_pages + i], 0, 0, 0))
    vpg = lambda i: pl.BlockSpec((None, None, page * H_ATT, DV_ATT), lambda b, pt: (layer, pt[b * n_pages + i], 0, 0))
    vec = pl.BlockSpec((1, DH_ATT), lambda b, pt: (0, 0))
    grid_spec = pltpu.PrefetchScalarGridSpec(
        num_scalar_prefetch=1,
        grid=(db,),
        in_specs=([tok, tok, tok] + [kpg(i) for i in range(n_pages)] + [vpg(i) for i in range(n_pages)]
                  + [vec, vec, vec, vec, pl.BlockSpec((1, DV_ATT), lambda b, pt: (0, 0))]),
        out_specs=tok,
    )
    return pl.pallas_call(
        functools.partial(_paged_kernel, n_pages=n_pages, lam_init=lam_init),
        grid_spec=grid_spec,
        out_shape=jax.ShapeDtypeStruct((db, 1, D_VA), F32),
        compiler_params=_cparams(("parallel",)),
        name="paged",
    )(page_table.reshape(-1), q, kn, vn, *([cache_kt] * n_pages), *([cache_v2] * n_pages), *lam_rows, g)


def _mix_kernel(x_ref, y_ref, bonus_ref, ob_ref, gate_ref, lng_ref, lnb_ref, sel_ref, wpa_ref, wpb_ref, wo_ref,
                nm_ref, wup_ref, wdn_ref, nf_ref, xo_ref, *, d_model, ff_chunk, final):
    sel = sel_ref[...]
    y = y_ref[...]
    mu = _seg_sum2(y, sel) * (1.0 / N_RWKV)
    d = y - mu
    var = _seg_sum2(d * d, sel) * (1.0 / N_RWKV)
    o_a = d * lax.rsqrt(var + LNX_EPS) * lng_ref[...] + lnb_ref[...] + bonus_ref[...]
    gates = gate_ref[...]
    merged = (gates[:, :d_model] * jnp.dot(o_a.astype(BF16), wpa_ref[...], preferred_element_type=F32)
              + gates[:, d_model:] * jnp.dot(ob_ref[...].astype(BF16), wpb_ref[...], preferred_element_type=F32))
    x = x_ref[...] + jnp.dot(merged.astype(BF16), wo_ref[...], preferred_element_type=F32)
    h2 = (x * lax.rsqrt(jnp.mean(x * x, axis=-1, keepdims=True) + RMS_EPS) * nm_ref[...]).astype(BF16)
    d_ff = wup_ref.shape[1]
    for c in range(d_ff // ff_chunk):
        cs = slice(c * ff_chunk, (c + 1) * ff_chunk)
        u = jnp.maximum(jnp.dot(h2, wup_ref[:, cs], preferred_element_type=F32), 0.0)
        x = x + jnp.dot((u * u).astype(BF16), wdn_ref[cs, :], preferred_element_type=F32)
    if final:
        x = x * lax.rsqrt(jnp.mean(x * x, axis=-1, keepdims=True) + RMS_EPS) * nf_ref[...]
    xo_ref[...] = x


def _mix(x2, y2, bonus2, ob2, gates2, lw, sel512, norm_final, final):
    m, d = x2.shape
    d_ff = lw['w_up_b'].shape[1]
    tm = _pick(m, (256, 128))
    row = lambda n: pl.BlockSpec((tm, n), lambda i: (i, 0))
    cs = _const_spec
    return pl.pallas_call(
        functools.partial(_mix_kernel, d_model=d, ff_chunk=min(d_ff, 1024), final=final),
        grid=(m // tm,),
        in_specs=[row(d), row(D_RWKV), row(D_RWKV), row(D_VA), row(2 * d), cs((1, D_RWKV)), cs((1, D_RWKV)),
                  cs((D_RWKV, D_RWKV)), cs((D_RWKV, d)), cs((D_VA, d)), cs((d, d)), cs((1, d)),
                  cs((d, d_ff)), cs((d_ff, d)), cs((1, d))],
        out_specs=row(d),
        out_shape=jax.ShapeDtypeStruct((m, d), F32),
        compiler_params=_cparams(("parallel",)),
        name="mix",
    )(x2, y2, bonus2, ob2, gates2, lw['lnx_g'], lw['lnx_b'], sel512, lw['w_pa_b'], lw['w_pb_b'], lw['w_o_b'],
      lw['norm_mlp'], lw['w_up_b'], lw['w_down_b'], norm_final)


def _layer(x3, lw, sel512, lam_init, t_valid, prev3, s0_pairs, attend, norm_final, final):
    b, tp, d = x3.shape
    m = b * tp
    prompt = prev3 is None
    pr3, q, k, v, kb, vb, gates = _inproj(x3, lw['norm_mix'], lw['w_in_b'], BF16 if prompt else F32, prompt)
    r, w, k2, vv, kk, kka, bonus = _prep(pr3, prev3, lw, sel512)
    if prev3 is not None:
        seqs = lambda a: a.reshape(m, 1, D_RWKV)
        y, s_fin = _wkv(seqs(r), seqs(w), seqs(k2), seqs(vv), seqs(kk), seqs(kka), s0_pairs, 1)
    else:
        y, s_fin = _wkv(r, w, k2, vv, kk, kka, s0_pairs, t_valid)
    ob = attend(q, k, v, kb, vb)
    xo = _mix(x3.reshape(m, d), y.reshape(m, D_RWKV), bonus.reshape(m, D_RWKV), ob.reshape(m, D_VA),
              gates.reshape(m, 2 * d), lw, sel512, norm_final, final)
    return xo.reshape(b, tp, d), k, v, s_fin, pr3


def kernel(x_prompt, x_sample, cache_k, cache_v, state_wkv, state_shift, page_table, meta_tokens, norm_mix, w_in,
           shift_mu, w0, w_w2, a0, w_a2, k_k, k_a, r_k, lnx_g, lnx_b, lambda_q1, lambda_k1, lambda_q2, lambda_k2,
           subln_g, w_pa, w_pb, w_o, norm_mlp, w_up, w_down, norm_final):
    bsz, seq, d = x_prompt.shape
    db = x_sample.shape[0]
    depth = w_in.shape[0]
    n_meta = meta_tokens.shape[0]
    t = seq + n_meta
    tp = -(-t // LANES) * LANES
    n_pool, page = cache_k.shape[1], cache_k.shape[2]

    xp = jnp.pad(x_prompt, ((0, 0), (n_meta, tp - t), (0, 0)))
    xp = lax.dynamic_update_slice(
        xp, jnp.broadcast_to(meta_tokens[None].astype(x_prompt.dtype), (bsz, n_meta, d)), (0, 0, 0))
    xs = x_sample.reshape(1, db, d)
    cache_kt = cache_k.transpose(0, 1, 3, 4, 2)
    cache_v2 = cache_v.reshape(depth, n_pool, page * H_ATT, DV_ATT)
    sel512 = _seg_ones(D_RWKV, N_RWKV).astype(BF16)
    nf = norm_final.reshape(1, d)
    zero_state = jnp.zeros((bsz, D_RWKV // WKV_W, N_RWKV, WKV_W), F32)

    kp_l, vp_l, wp_l, sp_l, ks_l, vs_l, ws_l, ss_l = ([] for _ in range(8))
    for l in range(depth):
        row = lambda a: a[l].reshape(1, -1)
        lw = {
            'norm_mix': norm_mix[l], 'w_in_b': w_in[l].astype(BF16), 'shift_mu': row(shift_mu), 'w0': row(w0),
            'w_w2': w_w2[l], 'a0': row(a0), 'w_a2': w_a2[l], 'k_k': row(k_k), 'k_a': row(k_a), 'r_k': row(r_k),
            'lnx_g': row(lnx_g), 'lnx_b': row(lnx_b), 'w_pa_b': w_pa[l].astype(BF16),
            'w_pb_b': w_pb[l].astype(BF16), 'w_o_b': w_o[l].astype(BF16), 'norm_mlp': row(norm_mlp),
            'w_up_b': w_up[l].astype(BF16), 'w_down_b': w_down[l].astype(BF16),
        }
        lam_init = 0.8 - 0.6 * math.exp(-0.3 * l)
        lam_rows = (row(lambda_q1), row(lambda_k1), row(lambda_q2), row(lambda_k2))
        g = row(subln_g)
        final = l == depth - 1

        attend_p = lambda q, k, v, kb, vb: _flash(q, kb, vb, lam_rows, g, lam_init)
        xp, k_new, v_new, s_fin, pr3 = _layer(xp, lw, sel512, lam_init, t, None, zero_state, attend_p, nf, final)
        kp_l.append(k_new.reshape(bsz, 2 * H_ATT, DH_ATT, tp)[..., :t].transpose(0, 3, 1, 2))
        vp_l.append(v_new.reshape(bsz, tp, H_ATT, DV_ATT)[:, :t])
        wp_l.append(_from_groups(s_fin))
        sp_l.append(pr3[:, t - 1])

        attend_s = lambda q, k, v, kb, vb: _paged(
            q.reshape(db, 1, D_QK), k.reshape(db, 1, D_QK), v.reshape(db, 1, D_VA), cache_kt, cache_v2, l,
            page_table, lam_rows, g, lam_init).reshape(1, db, D_VA)
        xs, k_new, v_new, s_fin, pr3 = _layer(xs, lw, sel512, lam_init, 1, state_shift[l][None],
                                              _to_groups(state_wkv[l]), attend_s, nf, final)
        ks_l.append(k_new.reshape(db, 1, 2 * H_ATT, DH_ATT))
        vs_l.append(v_new.reshape(db, 1, H_ATT, DV_ATT))
        ws_l.append(_from_groups(s_fin).astype(state_wkv.dtype))
        ss_l.append(pr3[0])

    y_prompt = xp[:, n_meta:t]
    y_sample = xs.reshape(db, 1, d)
    return (y_prompt, y_sample, jnp.stack(kp_l), jnp.stack(vp_l), jnp.stack(wp_l), jnp.stack(sp_l),
            jnp.stack(ks_l), jnp.stack(vs_l), jnp.stack(ws_l), jnp.stack(ss_l))
```

```python
import functools
import math

import jax
import jax.numpy as jnp
from jax import lax
from jax.experimental import pallas as pl
from jax.experimental.pallas import tpu as pltpu

F32 = jnp.float32
BF16 = jnp.bfloat16
HI = lax.Precision.HIGHEST

H_RWKV = 8
N_RWKV = 64
D_RWKV = H_RWKV * N_RWKV
LORA = 64
RWKV_COLS = 3 * D_RWKV + 2 * LORA
DECAY_SCALE = 0.606531
LNX_EPS = 64e-5
H_ATT = 4
DH_ATT = 64
DV_ATT = 2 * DH_ATT
D_QK = 2 * H_ATT * DH_ATT
D_VA = H_ATT * DV_ATT
ATT_SCALE = DH_ATT ** -0.5
NEG = -1e30
RMS_EPS = 1e-6

LANES = 128
SEQ_TILES = (384, 256, 128)
VMEM_LIMIT = 56 * 1024 * 1024


def _cparams(sem):
    return pltpu.CompilerParams(dimension_semantics=sem, vmem_limit_bytes=VMEM_LIMIT)


def _pick(n, cands):
    for c in cands:
        if n % c == 0:
            return c
    return n


def _const_spec(shape):
    nd = len(shape)
    return pl.BlockSpec(shape, lambda *_: (0,) * nd, pipeline_mode=pl.Buffered(1))


def _seg_ones(n, seg, scale=1.0):
    i = lax.broadcasted_iota(jnp.int32, (n, n), 0) // seg
    j = lax.broadcasted_iota(jnp.int32, (n, n), 1) // seg
    return jnp.where(i == j, scale, 0.0).astype(F32)


def _seg_sum2(x, sel_b):
    hi = x.astype(BF16)
    lo = (x - hi.astype(F32)).astype(BF16)
    o = jnp.dot(jnp.concatenate([hi, lo], axis=0), sel_b, preferred_element_type=F32)
    return o[:x.shape[0]] + o[x.shape[0]:]


def _inproj_kernel(x_ref, g_ref, w_ref, pr_ref, q_ref, k_ref, v_ref, kb_ref, vb_ref, gate_ref, *, d_model,
                   cache_layout):
    x = x_ref[0]
    hn = x * lax.rsqrt(jnp.mean(x * x, axis=-1, keepdims=True) + RMS_EPS) * g_ref[...]
    hb = hn.astype(BF16)
    tm = x.shape[0]

    def proj(lo, n):
        return jnp.dot(hb, w_ref[:, lo:lo + n], preferred_element_type=F32)

    q0 = RWKV_COLS
    k0 = q0 + D_QK
    v0 = k0 + D_QK
    g0 = v0 + D_VA
    pr_ref[0] = proj(0, RWKV_COLS)
    q_ref[0] = (proj(q0, D_QK) * ATT_SCALE).astype(q_ref.dtype)
    k = proj(k0, D_QK)
    kb_ref[0] = k.astype(BF16)
    v = proj(v0, D_VA)
    vb_ref[0] = v.astype(BF16)
    if cache_layout:
        k_ref[0] = k.T
        for h in range(H_ATT):
            v_ref[0, pl.ds(h, tm, stride=H_ATT), :] = v[:, DV_ATT * h:DV_ATT * (h + 1)]
    else:
        k_ref[0] = k
        v_ref[0] = v
    gate_ref[0] = jax.nn.sigmoid(proj(g0, 2 * d_model))


def _inproj(x3, g, w_bf, q_dtype, cache_layout):
    b, tp, d = x3.shape
    d_in = w_bf.shape[1]
    tm = _pick(tp, SEQ_TILES)
    row = lambda n: pl.BlockSpec((1, tm, n), lambda i, j: (i, j, 0))
    outs = [(RWKV_COLS, F32), (D_QK, q_dtype), (D_QK, F32), (D_VA, F32), (D_QK, BF16), (D_VA, BF16), (2 * d, F32)]
    specs = [row(n) for n, _ in outs]
    shapes = [jax.ShapeDtypeStruct((b, tp, n), dt) for n, dt in outs]
    if cache_layout:
        specs[2] = pl.BlockSpec((1, D_QK, tm), lambda i, j: (i, 0, j))
        shapes[2] = jax.ShapeDtypeStruct((b, D_QK, tp), F32)
        specs[3] = pl.BlockSpec((1, tm * H_ATT, DV_ATT), lambda i, j: (i, j, 0))
        shapes[3] = jax.ShapeDtypeStruct((b, tp * H_ATT, DV_ATT), F32)
    return pl.pallas_call(
        functools.partial(_inproj_kernel, d_model=d, cache_layout=cache_layout),
        grid=(b, tp // tm),
        in_specs=[row(d), _const_spec((1, d)), _const_spec((d, d_in))],
        out_specs=specs,
        out_shape=shapes,
        compiler_params=_cparams(("parallel", "parallel")),
        name="inproj",
    )(x3, g.reshape(1, d), w_bf)


def _prep_kernel(p_ref, prev_ref, mu_ref, w0_ref, ww2_ref, a0_ref, wa2_ref, kk_ref, ka_ref, rk_ref, sel_ref,
                 r_out, w_out, k_out, v_out, kk_out, kka_out, bonus_out, *, shift_rows):
    p = p_ref[0]
    if shift_rows:
        first = pl.program_id(1) == 0
        before = jnp.where(first, 0.0, prev_ref[0][7:8])
        rolled = pltpu.roll(p, 1, axis=0)
        rows = lax.broadcasted_iota(jnp.int32, p.shape, 0)
        prev = jnp.where(rows == 0, before, rolled)
    else:
        prev = prev_ref[0]
    m = p + (prev - p) * mu_ref[...]
    r = m[:, :D_RWKV]
    k = m[:, D_RWKV:2 * D_RWKV]
    v = m[:, 2 * D_RWKV:3 * D_RWKV]
    w_lo = m[:, 3 * D_RWKV:3 * D_RWKV + LORA]
    a_lo = m[:, 3 * D_RWKV + LORA:]
    sel = sel_ref[...]
    w = jnp.exp(-DECAY_SCALE * jax.nn.sigmoid(
        w0_ref[...] + jnp.dot(jnp.tanh(w_lo), ww2_ref[...], precision=HI, preferred_element_type=F32)))
    a = jax.nn.sigmoid(a0_ref[...] + jnp.dot(a_lo, wa2_ref[...], precision=HI, preferred_element_type=F32))
    kk = k * kk_ref[...]
    k2 = k * (1.0 + (a - 1.0) * ka_ref[...])
    ss = _seg_sum2(kk * kk, sel)
    kkn = kk / jnp.maximum(jnp.sqrt(ss), 1e-12)
    r_out[0] = r
    w_out[0] = w
    k_out[0] = k2
    v_out[0] = v
    kk_out[0] = kkn
    kka_out[0] = kkn * a
    bonus_out[0] = _seg_sum2(r * k2 * rk_ref[...], sel) * v


def _prep(p3, prev3, lw, sel512):
    b, tp, c = p3.shape
    shift_rows = prev3 is None
    tt = _pick(tp, SEQ_TILES)
    cur = pl.BlockSpec((1, tt, c), lambda i, j: (i, j, 0))
    if shift_rows:
        nb8 = tt // 8
        prev_spec = pl.BlockSpec((1, 8, c), lambda i, j: (i, jnp.maximum(j * nb8 - 1, 0), 0))
        prev_arr = p3
    else:
        prev_spec = cur
        prev_arr = prev3
    out = pl.BlockSpec((1, tt, D_RWKV), lambda i, j: (i, j, 0))
    vec = lambda n: _const_spec((1, n))
    return pl.pallas_call(
        functools.partial(_prep_kernel, shift_rows=shift_rows),
        grid=(b, tp // tt),
        in_specs=[cur, prev_spec, vec(c), vec(D_RWKV), _const_spec((LORA, D_RWKV)), vec(D_RWKV),
                  _const_spec((LORA, D_RWKV)), vec(D_RWKV), vec(D_RWKV), vec(D_RWKV),
                  _const_spec((D_RWKV, D_RWKV))],
        out_specs=[out] * 7,
        out_shape=[jax.ShapeDtypeStruct((b, tp, D_RWKV), F32)] * 7,
        compiler_params=_cparams(("parallel", "parallel")),
        name="prep",
    )(p3, prev_arr, lw['shift_mu'], lw['w0'], lw['w_w2'], lw['a0'], lw['w_a2'], lw['k_k'], lw['k_a'],
      lw['r_k'], sel512)


WKV_W = 4 * N_RWKV


def _wkv_kernel(r_ref, w_ref, k_ref, v_ref, kk_ref, kka_ref, s0_ref, y_ref, sT_ref, s_scr, ybuf, vb_scr, *, t_valid,
                nb, tc, gsz):
    c = pl.program_id(1)
    nq = D_RWKV // WKV_W

    @pl.when(c == 0)
    def _():
        s_scr[...] = s0_ref[...].reshape(s_scr.shape)

    nsteps = jnp.clip(t_valid - c * tc, 0, tc)

    @pl.when(nsteps < tc)
    def _():
        y_ref[...] = jnp.zeros_like(y_ref)

    sel = _seg_ones(WKV_W, N_RWKV).astype(BF16)
    eye = jnp.where(lax.broadcasted_iota(jnp.int32, (N_RWKV, WKV_W), 0)
                    == lax.broadcasted_iota(jnp.int32, (N_RWKV, WKV_W), 1) % N_RWKV, 1.0, 0.0).astype(F32)[None]
    eye_b = eye.astype(BF16)
    lane8 = lax.broadcasted_iota(jnp.int32, (8, WKV_W), 1)
    lane_blk = (lane8 % N_RWKV) // 8
    eye8 = jnp.where(lax.broadcasted_iota(jnp.int32, (8, WKV_W), 0) == lane8 % 8, 1.0, 0.0).astype(F32)
    chains = [(b, q) for b in range(nb) for q in range(nq)]

    nch = len(chains)
    tile = (nch, N_RWKV, WKV_W)

    def seg_sums(*xs):
        lhs = jnp.concatenate([x.astype(BF16).reshape(-1, WKV_W) for x in xs], axis=0)
        o = jnp.dot(lhs, sel, preferred_element_type=F32)
        n = nch * N_RWKV
        return [o[j * n:(j + 1) * n].reshape(tile) for j in range(len(xs))]

    def group(g, carry):
        t0 = pl.multiple_of(g * gsz, gsz)

        def rows(ref, i):
            return jnp.stack([ref[b, pl.ds(t0, gsz), WKV_W * q:WKV_W * (q + 1)][i:i + 1] for b, q in chains])

        vbs = seg_sums(*[rows(v_ref, i).astype(BF16) * eye_b for i in range(gsz)])
        for i in range(gsz):
            vb_scr[i] = vbs[i]
        s = s_scr[...]
        sk, = seg_sums(s * rows(kk_ref, 0))
        for i in range(gsz):
            s = s * rows(w_ref, i) - sk * rows(kka_ref, i) + vb_scr[i] * rows(k_ref, i)
            if i + 1 < gsz:
                sk, yb = seg_sums(s * rows(kk_ref, i + 1), s * rows(r_ref, i))
            else:
                yb, = seg_sums(s * rows(r_ref, i))
            yb4 = yb.reshape(nch, N_RWKV // 8, 8, WKV_W)
            diag = yb4[:, 0]
            for blk in range(1, N_RWKV // 8):
                diag = jnp.where(lane_blk == blk, yb4[:, blk], diag)
            ybuf[:, i:i + 1, :] = jnp.sum(diag * eye8, axis=1, keepdims=True)
        s_scr[...] = s
        for n, (b, q) in enumerate(chains):
            y_ref[b, pl.ds(t0, gsz), WKV_W * q:WKV_W * (q + 1)] = ybuf[n]
        return carry

    lax.fori_loop(0, nsteps // gsz, group, 0)

    @pl.when(c == pl.num_programs(1) - 1)
    def _():
        sT_ref[...] = s_scr[...].reshape(sT_ref.shape)


WKV_G = 8
WKV_UNROLL = 2


def _wkv_group_kernel(r_ref, w_ref, k_ref, v_ref, kk_ref, kka_ref, s0_ref, y_ref, sT_ref, s_scr, *, t_valid, nb, tc):
    c = pl.program_id(1)
    nq = D_RWKV // WKV_W
    g_sz = WKV_G
    chains = [(b, q) for b in range(nb) for q in range(nq)]
    nch = len(chains)
    heads = WKV_W // N_RWKV

    @pl.when(c == 0)
    def _():
        s_scr[...] = s0_ref[...].reshape(s_scr.shape)

    nsteps = jnp.clip(t_valid - c * tc, 0, tc)

    @pl.when(nsteps < tc)
    def _():
        y_ref[...] = jnp.zeros_like(y_ref)

    sel = _seg_ones(WKV_W, N_RWKV).astype(BF16)
    row8 = lax.broadcasted_iota(jnp.int32, (g_sz, WKV_W), 0)
    head16 = lax.broadcasted_iota(jnp.int32, (2 * g_sz, WKV_W), 1) // N_RWKV
    head64 = lax.broadcasted_iota(jnp.int32, (N_RWKV, WKV_W), 1) // N_RWKV

    def by_step(tiles, i):
        return jnp.concatenate([t[i:i + 1] for t in tiles], axis=0)

    pairs = [(kind, t, s) for t in range(g_sz) for s in range(t + 1) for kind in 'abcd' if s < t or kind in 'cd']
    pair_index = {p: i for i, p in enumerate(pairs)}

    def prelude(t0):
        tile = lambda ref, b, q: ref[b, pl.ds(t0, g_sz), WKV_W * q:WKV_W * (q + 1)]

        kap_p, rh_p, kh_p, bh_p, v_p, gam_p = [], [], [], [], [], []
        for b, q in chains:
            w = tile(w_ref, b, q)
            gam, shift = w, 1
            while shift < g_sz:
                gam = gam * jnp.where(row8 >= shift, pltpu.roll(gam, shift, axis=0), 1.0)
                shift *= 2
            gprev = jnp.where(row8 >= 1, pltpu.roll(gam, 1, axis=0), 1.0)
            ginv = 1.0 / gam
            kap_p.append(tile(kk_ref, b, q) * gprev)
            rh_p.append(tile(r_ref, b, q) * gam)
            kh_p.append(tile(k_ref, b, q) * ginv)
            bh_p.append(tile(kka_ref, b, q) * ginv)
            v_p.append(tile(v_ref, b, q))
            gam_p.append(gam)

        kap_s = [by_step(kap_p, i) for i in range(g_sz)]
        rh_s = [by_step(rh_p, i) for i in range(g_sz)]
        kh_s = [by_step(kh_p, i) for i in range(g_sz)]
        bh_s = [by_step(bh_p, i) for i in range(g_sz)]
        v_s = [by_step(v_p, i) for i in range(g_sz)]

        prods = [(kap_s[t] if kind in 'ab' else rh_s[t]) * (kh_s[s] if kind in 'ac' else bh_s[s])
                 for kind, t, s in pairs]
        dots = jnp.dot(jnp.concatenate(prods, axis=0).astype(BF16), sel, preferred_element_type=F32)
        coef = lambda kind, t, s: dots[pair_index[kind, t, s] * nch:(pair_index[kind, t, s] + 1) * nch]
        return kap_p, rh_p, kh_p, bh_p, v_p, gam_p, v_s, coef

    def chain(t0, pre):
        kap_p, rh_p, kh_p, bh_p, v_p, gam_p, v_s, coef = pre

        u0_p, y0_p = [], []
        for n in range(nch):
            sb = s_scr[n].astype(BF16)
            lhs = jnp.concatenate([kap_p[n], rh_p[n]], axis=0)
            lhs4 = jnp.concatenate([jnp.where(head16 == h, lhs, 0.0) for h in range(heads)], axis=0).astype(BF16)
            out = lax.dot_general(lhs4, jnp.concatenate([sb] * heads, axis=0), (((1,), (1,)), ((), ())),
                                  preferred_element_type=F32)
            uy = out[:2 * g_sz]
            for h in range(1, heads):
                uy = jnp.where(head16 == h, out[2 * g_sz * h:2 * g_sz * (h + 1)], uy)
            u0_p.append(uy[:g_sz])
            y0_p.append(uy[g_sz:])

        us, ys = [], []
        for t in range(g_sz):
            u = by_step(u0_p, t)
            y = by_step(y0_p, t)
            for s in range(t):
                u = u + coef('a', t, s) * v_s[s] - coef('b', t, s) * us[s]
            us.append(u)
            for s in range(t + 1):
                y = y + coef('c', t, s) * v_s[s] - coef('d', t, s) * us[s]
            ys.append(y)

        for n, (b, q) in enumerate(chains):
            u_n = by_step(us, n)
            x = jnp.concatenate([v_p[n], -u_n], axis=0).astype(BF16)
            z = jnp.concatenate([kh_p[n], bh_p[n]], axis=0).astype(BF16)
            full = lax.dot_general(x, z, (((0,), (0,)), ((), ())), preferred_element_type=F32)
            ds = full[:N_RWKV]
            for h in range(1, heads):
                ds = jnp.where(head64 == h, full[N_RWKV * h:N_RWKV * (h + 1)], ds)
            s_scr[n] = (s_scr[n] + ds) * gam_p[n][g_sz - 1:g_sz]
            y_ref[b, pl.ds(t0, g_sz), WKV_W * q:WKV_W * (q + 1)] = by_step(ys, n)

    def trip(i, carry):
        for sub in range(WKV_UNROLL):
            t0 = pl.multiple_of(i * (WKV_UNROLL * g_sz) + sub * g_sz, g_sz)
            chain(t0, prelude(t0))
        return carry

    lax.fori_loop(0, nsteps // (WKV_UNROLL * g_sz), trip, 0)

    @pl.when(c == pl.num_programs(1) - 1)
    def _():
        sT_ref[...] = s_scr[...].reshape(sT_ref.shape)


def _wkv(r, w, k, v, kk, kka, s0, t_valid):
    b, tp, _ = r.shape
    nq = D_RWKV // WKV_W
    nb = _pick(b, (4, 2, 1))
    tc = _pick(tp, (128,))
    grouped = tc % WKV_G == 0
    gsz = WKV_G if grouped else tc
    trip = WKV_G * WKV_UNROLL if grouped else tc
    assert t_valid % trip == 0 and tc % trip == 0, "sequence length must fill whole loop trips"
    seq = pl.BlockSpec((nb, tc, D_RWKV), lambda i, c: (i, c, 0))
    st = pl.BlockSpec((nb, nq, N_RWKV, WKV_W), lambda i, c: (i, 0, 0, 0))
    state = pltpu.VMEM((nb * nq, N_RWKV, WKV_W), F32)
    if grouped:
        body = functools.partial(_wkv_group_kernel, t_valid=t_valid, nb=nb, tc=tc)
        scratch = [state]
    else:
        body = functools.partial(_wkv_kernel, t_valid=t_valid, nb=nb, tc=tc, gsz=gsz)
        scratch = [state, pltpu.VMEM((nb * nq, gsz, WKV_W), F32), pltpu.VMEM((gsz, nb * nq, N_RWKV, WKV_W), F32)]
    return pl.pallas_call(
        body,
        grid=(b // nb, tp // tc),
        in_specs=[seq] * 6 + [st],
        out_specs=[seq, st],
        out_shape=[jax.ShapeDtypeStruct((b, tp, D_RWKV), F32),
                   jax.ShapeDtypeStruct((b, nq, N_RWKV, WKV_W), F32)],
        scratch_shapes=scratch,
        compiler_params=_cparams(("parallel", "arbitrary")),
        name="wkv",
    )(r, w, k, v, kk, kka, s0)


def _to_groups(s):
    b = s.shape[0]
    nq = D_RWKV // WKV_W
    return s.reshape(b, nq, H_RWKV // nq, N_RWKV, N_RWKV).transpose(0, 1, 3, 2, 4).reshape(b, nq, N_RWKV, WKV_W)


def _from_groups(s):
    b = s.shape[0]
    nq = D_RWKV // WKV_W
    return s.reshape(b, nq, N_RWKV, H_RWKV // nq, N_RWKV).transpose(0, 1, 3, 2, 4).reshape(
        b, H_RWKV, N_RWKV, N_RWKV)


def _lam(lq1, lk1, lq2, lk2, lam_init):
    return (jnp.exp(jnp.sum(lq1 * lk1, axis=-1, keepdims=True))
            - jnp.exp(jnp.sum(lq2 * lk2, axis=-1, keepdims=True)) + lam_init)


def _subln(o, g, lam_init):
    return o * lax.rsqrt(jnp.mean(o * o, axis=-1, keepdims=True) + RMS_EPS) * g * (1.0 - lam_init)


def _flash_kernel(q_ref, k_ref, v_ref, lq1, lk1, lq2, lk2, g_ref, o_ref, q2_scr, m_scr, l_scr, acc_scr, *, lam_init,
                  tq):
    qi = pl.program_id(1)
    m_scr[...] = jnp.full_like(m_scr, NEG)
    l_scr[...] = jnp.zeros_like(l_scr)
    acc_scr[...] = jnp.zeros_like(acc_scr)
    lane = lax.broadcasted_iota(jnp.int32, (tq, LANES), 1)
    for h in range(H_ATT):
        qh = q_ref[0, :, LANES * h:LANES * (h + 1)]
        q2_scr[h, :tq] = jnp.where(lane < DH_ATT, qh, jnp.zeros_like(qh))
        q2_scr[h, tq:] = jnp.where(lane >= DH_ATT, qh, jnp.zeros_like(qh))

    def block(k0, tk, masked):
        if masked:
            tri = lax.broadcasted_iota(jnp.int32, (tq, tk), 1) <= lax.broadcasted_iota(jnp.int32, (tq, tk), 0)
            keep = jnp.concatenate([tri, tri], axis=0)
        ones = jnp.ones((tk, LANES), BF16)
        for h in range(H_ATT):
            ln = slice(LANES * h, LANES * (h + 1))
            kh = k_ref[0, pl.ds(k0, tk), ln]
            v1 = jnp.concatenate([v_ref[0, pl.ds(k0, tk), ln], ones], axis=1)
            s = lax.dot_general(q2_scr[h], kh, (((1,), (1,)), ((), ())), preferred_element_type=F32)
            if masked:
                s = jnp.where(keep, s, NEG)
            m_prev = m_scr[h]
            m_new = jnp.maximum(m_prev, jnp.max(s, axis=-1, keepdims=True))
            alpha = jnp.exp(m_prev - m_new)
            p = jnp.exp((s - jnp.concatenate([m_new] * (tk // LANES), axis=1)).astype(BF16))
            pv = jnp.dot(p, v1, preferred_element_type=F32)
            acc_scr[h] = alpha * acc_scr[h] + pv[:, :LANES]
            l_scr[h] = alpha * l_scr[h] + pv[:, LANES:]
            m_scr[h] = m_new

    def body(kj, carry):
        block(pl.multiple_of(kj * (2 * tq), tq), 2 * tq, False)
        return carry

    lax.fori_loop(0, qi // 2, body, 0)

    @pl.when(qi % 2 == 1)
    def _():
        block(pl.multiple_of((qi - 1) * tq, tq), tq, False)

    block(pl.multiple_of(qi * tq, tq), tq, True)
    lam = _lam(lq1[...], lk1[...], lq2[...], lk2[...], lam_init)
    for h in range(H_ATT):
        o = acc_scr[h] / l_scr[h]
        o_ref[0, :, LANES * h:LANES * (h + 1)] = _subln(o[:tq] - lam * o[tq:], g_ref[...], lam_init)


def _flash(qb, kb, vb, lam_rows, g, lam_init):
    b, tp, _ = qb.shape
    tq = _pick(tp, SEQ_TILES)
    qspec = pl.BlockSpec((1, tq, D_QK), lambda i, q: (i, q, 0))
    kspec = pl.BlockSpec((1, tp, D_QK), lambda i, q: (i, 0, 0))
    vec = _const_spec((1, DH_ATT))
    return pl.pallas_call(
        functools.partial(_flash_kernel, lam_init=lam_init, tq=tq),
        grid=(b, tp // tq),
        in_specs=[qspec, kspec, kspec, vec, vec, vec, vec, _const_spec((1, DV_ATT))],
        out_specs=qspec,
        out_shape=jax.ShapeDtypeStruct((b, tp, D_VA), F32),
        scratch_shapes=[pltpu.VMEM((H_ATT, 2 * tq, LANES), BF16)] + [pltpu.VMEM((H_ATT, 2 * tq, LANES), F32)] * 3,
        compiler_params=_cparams(("parallel", "arbitrary")),
        name="flash",
    )(qb, kb, vb, *lam_rows, g)


PAGED_ROWS = 16


def _paged_kernel(pt_ref, q_ref, kn_ref, vn_ref, *rest, n_pages, lam_init):
    k_refs, v_refs = rest[:n_pages], rest[n_pages:2 * n_pages]
    lq1, lk1, lq2, lk2, g_ref, o_ref = rest[2 * n_pages:]
    page = v_refs[0].shape[0] // H_ATT
    rows = lax.broadcasted_iota(jnp.int32, (PAGED_ROWS, D_QK), 0)
    cols = lax.broadcasted_iota(jnp.int32, (PAGED_ROWS, D_QK), 1) // DH_ATT
    q8 = jnp.where(rows == cols, q_ref[0], 0.0)
    q_hi = q8.astype(BF16)
    q2 = jnp.concatenate([q_hi, (q8 - q_hi.astype(F32)).astype(BF16)], axis=0)
    scores = []
    for i in range(n_pages):
        kt = k_refs[i][...].reshape(D_QK, page).astype(BF16)
        s2 = jnp.dot(q2, kt, preferred_element_type=F32)
        scores.append(s2[:PAGED_ROWS] + s2[PAGED_ROWS:])
    s_self = jnp.sum(q8 * kn_ref[0], axis=-1, keepdims=True)
    m = s_self
    for s in scores:
        m = jnp.maximum(m, jnp.max(s, axis=-1, keepdims=True))
    p_self = jnp.exp(s_self - m)
    l = p_self
    acc = [p_self * vn_ref[0][:, LANES * h:LANES * (h + 1)] for h in range(H_ATT)]
    for i in range(n_pages):
        p = jnp.exp(scores[i] - m)
        l = l + jnp.sum(p, axis=-1, keepdims=True)
        pb = p.astype(BF16)
        for h in range(H_ATT):
            vh = v_refs[i][pl.ds(h, page, stride=H_ATT), :].astype(BF16)
            acc[h] = acc[h] + jnp.dot(pb, vh, preferred_element_type=F32)
    lam = _lam(lq1[...], lk1[...], lq2[...], lk2[...], lam_init)
    for h in range(H_ATT):
        o = acc[h] / l
        d = o[2 * h:2 * h + 1] - lam * o[2 * h + 1:2 * h + 2]
        o_ref[0, :, LANES * h:LANES * (h + 1)] = _subln(d, g_ref[...], lam_init)


def _paged(q, kn, vn, cache_kt, cache_v2, layer, page_table, lam_rows, g, lam_init):
    db = q.shape[0]
    n_pages = page_table.shape[1]
    page = cache_kt.shape[-1]
    tok = pl.BlockSpec((1, 1, D_QK), lambda b, pt: (b, 0, 0))
    kpg = lambda i: pl.BlockSpec((None, None, 2 * H_ATT, DH_ATT, page),
                                 lambda b, pt: (layer, pt[b * n_pages + i], 0, 0, 0))
    vpg = lambda i: pl.BlockSpec((None, None, page * H_ATT, DV_ATT), lambda b, pt: (layer, pt[b * n_pages + i], 0, 0))
    vec = pl.BlockSpec((1, DH_ATT), lambda b, pt: (0, 0))
    grid_spec = pltpu.PrefetchScalarGridSpec(
        num_scalar_prefetch=1,
        grid=(db,),
        in_specs=([tok, tok, tok] + [kpg(i) for i in range(n_pages)] + [vpg(i) for i in range(n_pages)]
                  + [vec, vec, vec, vec, pl.BlockSpec((1, DV_ATT), lambda b, pt: (0, 0))]),
        out_specs=tok,
    )
    return pl.pallas_call(
        functools.partial(_paged_kernel, n_pages=n_pages, lam_init=lam_init),
        grid_spec=grid_spec,
        out_shape=jax.ShapeDtypeStruct((db, 1, D_VA), F32),
        compiler_params=_cparams(("parallel",)),
        name="paged",
    )(page_table.reshape(-1), q, kn, vn, *([cache_kt] * n_pages), *([cache_v2] * n_pages), *lam_rows, g)


def _mix_kernel(x_ref, y_ref, bonus_ref, ob_ref, gate_ref, lng_ref, lnb_ref, wpa_ref, wpb_ref, wo_ref,
                nm_ref, wup_ref, wdn_ref, nf_ref, xo_ref, *, d_model, ff_chunk, final):
    y = y_ref[...]
    low = lax.broadcasted_iota(jnp.int32, (y.shape[0], LANES), 1) < N_RWKV

    def head_mean(z):
        blocks = []
        for j in range(D_RWKV // LANES):
            zb = z[:, LANES * j:LANES * (j + 1)]
            lo = jnp.sum(jnp.where(low, zb, 0.0), axis=-1, keepdims=True)
            hi = jnp.sum(jnp.where(low, 0.0, zb), axis=-1, keepdims=True)
            blocks.append(jnp.where(low, lo, hi))
        return jnp.concatenate(blocks, axis=1) * (1.0 / N_RWKV)

    d = y - head_mean(y)
    var = head_mean(d * d)
    o_a = d * lax.rsqrt(var + LNX_EPS) * lng_ref[...] + lnb_ref[...] + bonus_ref[...]
    gates = gate_ref[...]
    merged = (gates[:, :d_model] * jnp.dot(o_a.astype(BF16), wpa_ref[...], preferred_element_type=F32)
              + gates[:, d_model:] * jnp.dot(ob_ref[...].astype(BF16), wpb_ref[...], preferred_element_type=F32))
    x = x_ref[...] + jnp.dot(merged.astype(BF16), wo_ref[...], preferred_element_type=F32)
    h2 = (x * lax.rsqrt(jnp.mean(x * x, axis=-1, keepdims=True) + RMS_EPS) * nm_ref[...]).astype(BF16)
    d_ff = wup_ref.shape[1]
    for c in range(d_ff // ff_chunk):
        cs = slice(c * ff_chunk, (c + 1) * ff_chunk)
        u = jnp.maximum(jnp.dot(h2, wup_ref[:, cs], preferred_element_type=F32), 0.0)
        x = x + jnp.dot((u * u).astype(BF16), wdn_ref[cs, :], preferred_element_type=F32)
    if final:
        x = x * lax.rsqrt(jnp.mean(x * x, axis=-1, keepdims=True) + RMS_EPS) * nf_ref[...]
    xo_ref[...] = x


def _mix(x2, y2, bonus2, ob2, gates2, lw, norm_final, final):
    m, d = x2.shape
    d_ff = lw['w_up_b'].shape[1]
    tm = _pick(m, (512, 256, 128))
    row = lambda n: pl.BlockSpec((tm, n), lambda i: (i, 0))
    cs = _const_spec
    return pl.pallas_call(
        functools.partial(_mix_kernel, d_model=d, ff_chunk=min(d_ff, 1024), final=final),
        grid=(m // tm,),
        in_specs=[row(d), row(D_RWKV), row(D_RWKV), row(D_VA), row(2 * d), cs((1, D_RWKV)), cs((1, D_RWKV)),
                  cs((D_RWKV, d)), cs((D_VA, d)), cs((d, d)), cs((1, d)),
                  cs((d, d_ff)), cs((d_ff, d)), cs((1, d))],
        out_specs=row(d),
        out_shape=jax.ShapeDtypeStruct((m, d), F32),
        compiler_params=_cparams(("parallel",)),
        name="mix",
    )(x2, y2, bonus2, ob2, gates2, lw['lnx_g'], lw['lnx_b'], lw['w_pa_b'], lw['w_pb_b'], lw['w_o_b'],
      lw['norm_mlp'], lw['w_up_b'], lw['w_down_b'], norm_final)


def _layer(x3, lw, sel512, lam_init, t_valid, prev3, s0_pairs, attend, norm_final, final):
    b, tp, d = x3.shape
    m = b * tp
    prompt = prev3 is None
    pr3, q, k, v, kb, vb, gates = _inproj(x3, lw['norm_mix'], lw['w_in_b'], BF16 if prompt else F32, prompt)
    r, w, k2, vv, kk, kka, bonus = _prep(pr3, prev3, lw, sel512)
    if prev3 is not None:
        seqs = lambda a: a.reshape(m, 1, D_RWKV)
        y, s_fin = _wkv(seqs(r), seqs(w), seqs(k2), seqs(vv), seqs(kk), seqs(kka), s0_pairs, 1)
    else:
        y, s_fin = _wkv(r, w, k2, vv, kk, kka, s0_pairs, t_valid)
    ob = attend(q, k, v, kb, vb)
    xo = _mix(x3.reshape(m, d), y.reshape(m, D_RWKV), bonus.reshape(m, D_RWKV), ob.reshape(m, D_VA),
              gates.reshape(m, 2 * d), lw, norm_final, final)
    return xo.reshape(b, tp, d), k, v, s_fin, pr3


def kernel(x_prompt, x_sample, cache_k, cache_v, state_wkv, state_shift, page_table, meta_tokens, norm_mix, w_in,
           shift_mu, w0, w_w2, a0, w_a2, k_k, k_a, r_k, lnx_g, lnx_b, lambda_q1, lambda_k1, lambda_q2, lambda_k2,
           subln_g, w_pa, w_pb, w_o, norm_mlp, w_up, w_down, norm_final):
    bsz, seq, d = x_prompt.shape
    db = x_sample.shape[0]
    depth = w_in.shape[0]
    n_meta = meta_tokens.shape[0]
    t = seq + n_meta
    tp = -(-t // LANES) * LANES
    n_pool, page = cache_k.shape[1], cache_k.shape[2]

    xp = jnp.pad(x_prompt, ((0, 0), (n_meta, tp - t), (0, 0)))
    xp = lax.dynamic_update_slice(
        xp, jnp.broadcast_to(meta_tokens[None].astype(x_prompt.dtype), (bsz, n_meta, d)), (0, 0, 0))
    xs = x_sample.reshape(1, db, d)
    cache_kt = cache_k.transpose(0, 1, 3, 4, 2)
    cache_v2 = cache_v.reshape(depth, n_pool, page * H_ATT, DV_ATT)
    sel512 = _seg_ones(D_RWKV, N_RWKV).astype(BF16)
    nf = norm_final.reshape(1, d)
    zero_state = jnp.zeros((bsz, D_RWKV // WKV_W, N_RWKV, WKV_W), F32)

    kp_l, vp_l, wp_l, sp_l, ks_l, vs_l, ws_l, ss_l = ([] for _ in range(8))
    for l in range(depth):
        row = lambda a: a[l].reshape(1, -1)
        lw = {
            'norm_mix': norm_mix[l], 'w_in_b': w_in[l].astype(BF16), 'shift_mu': row(shift_mu), 'w0': row(w0),
            'w_w2': w_w2[l], 'a0': row(a0), 'w_a2': w_a2[l], 'k_k': row(k_k), 'k_a': row(k_a), 'r_k': row(r_k),
            'lnx_g': row(lnx_g), 'lnx_b': row(lnx_b), 'w_pa_b': w_pa[l].astype(BF16),
            'w_pb_b': w_pb[l].astype(BF16), 'w_o_b': w_o[l].astype(BF16), 'norm_mlp': row(norm_mlp),
            'w_up_b': w_up[l].astype(BF16), 'w_down_b': w_down[l].astype(BF16),
        }
        lam_init = 0.8 - 0.6 * math.exp(-0.3 * l)
        lam_rows = (row(lambda_q1), row(lambda_k1), row(lambda_q2), row(lambda_k2))
        g = row(subln_g)
        final = l == depth - 1

        attend_p = lambda q, k, v, kb, vb: _flash(q, kb, vb, lam_rows, g, lam_init)
        xp, k_new, v_new, s_fin, pr3 = _layer(xp, lw, sel512, lam_init, t, None, zero_state, attend_p, nf, final)
        kp_l.append(k_new.reshape(bsz, 2 * H_ATT, DH_ATT, tp)[..., :t].transpose(0, 3, 1, 2))
        vp_l.append(v_new.reshape(bsz, tp, H_ATT, DV_ATT)[:, :t])
        wp_l.append(_from_groups(s_fin))
        sp_l.append(pr3[:, t - 1])

        attend_s = lambda q, k, v, kb, vb: _paged(
            q.reshape(db, 1, D_QK), k.reshape(db, 1, D_QK), v.reshape(db, 1, D_VA), cache_kt, cache_v2, l,
            page_table, lam_rows, g, lam_init).reshape(1, db, D_VA)
        xs, k_new, v_new, s_fin, pr3 = _layer(xs, lw, sel512, lam_init, 1, state_shift[l][None],
                                              _to_groups(state_wkv[l]), attend_s, nf, final)
        ks_l.append(k_new.reshape(db, 1, 2 * H_ATT, DH_ATT))
        vs_l.append(v_new.reshape(db, 1, H_ATT, DV_ATT))
        ws_l.append(_from_groups(s_fin).astype(state_wkv.dtype))
        ss_l.append(pr3[0])

    y_prompt = xp[:, n_meta:t]
    y_sample = xs.reshape(db, 1, d)
    return (y_prompt, y_sample, jnp.stack(kp_l), jnp.stack(vp_l), jnp.stack(wp_l), jnp.stack(sp_l),
            jnp.stack(ks_l), jnp.stack(vs_l), jnp.stack(ws_l), jnp.stack(ss_l))
```

```python
import functools
import math

import jax
import jax.numpy as jnp
from jax import lax
from jax.experimental import pallas as pl
from jax.experimental.pallas import tpu as pltpu

F32 = jnp.float32
BF16 = jnp.bfloat16
HI = lax.Precision.HIGHEST

H_RWKV = 8
N_RWKV = 64
D_RWKV = H_RWKV * N_RWKV
LORA = 64
RWKV_COLS = 3 * D_RWKV + 2 * LORA
DECAY_SCALE = 0.606531
LNX_EPS = 64e-5
H_ATT = 4
DH_ATT = 64
DV_ATT = 2 * DH_ATT
D_QK = 2 * H_ATT * DH_ATT
D_VA = H_ATT * DV_ATT
ATT_SCALE = DH_ATT ** -0.5
NEG = -1e30
RMS_EPS = 1e-6

LANES = 128
SEQ_TILES = (384, 256, 128)
VMEM_LIMIT = 56 * 1024 * 1024


def _cparams(sem):
    return pltpu.CompilerParams(dimension_semantics=sem, vmem_limit_bytes=VMEM_LIMIT)


def _pick(n, cands):
    for c in cands:
        if n % c == 0:
            return c
    return n


def _const_spec(shape):
    nd = len(shape)
    return pl.BlockSpec(shape, lambda *_: (0,) * nd, pipeline_mode=pl.Buffered(1))


def _seg_ones(n, seg, scale=1.0):
    i = lax.broadcasted_iota(jnp.int32, (n, n), 0) // seg
    j = lax.broadcasted_iota(jnp.int32, (n, n), 1) // seg
    return jnp.where(i == j, scale, 0.0).astype(F32)


def _seg_sum2(x, sel_b):
    hi = x.astype(BF16)
    lo = (x - hi.astype(F32)).astype(BF16)
    o = jnp.dot(jnp.concatenate([hi, lo], axis=0), sel_b, preferred_element_type=F32)
    return o[:x.shape[0]] + o[x.shape[0]:]


def _inproj_kernel(x_ref, g_ref, w_ref, pr_ref, q_ref, k_ref, v_ref, kb_ref, vb_ref, gate_ref, *, d_model,
                   cache_layout):
    x = x_ref[0]
    hn = x * lax.rsqrt(jnp.mean(x * x, axis=-1, keepdims=True) + RMS_EPS) * g_ref[...]
    hb = hn.astype(BF16)
    tm = x.shape[0]

    def proj(lo, n):
        return jnp.dot(hb, w_ref[:, lo:lo + n], preferred_element_type=F32)

    q0 = RWKV_COLS
    k0 = q0 + D_QK
    v0 = k0 + D_QK
    g0 = v0 + D_VA
    pr_ref[0] = proj(0, RWKV_COLS)
    q_ref[0] = (proj(q0, D_QK) * ATT_SCALE).astype(q_ref.dtype)
    k = proj(k0, D_QK)
    kb_ref[0] = k.astype(BF16)
    v = proj(v0, D_VA)
    vb_ref[0] = v.astype(BF16)
    if cache_layout:
        k_ref[0] = k.T
        for h in range(H_ATT):
            v_ref[0, pl.ds(h, tm, stride=H_ATT), :] = v[:, DV_ATT * h:DV_ATT * (h + 1)]
    else:
        k_ref[0] = k
        v_ref[0] = v
    gate_ref[0] = jax.nn.sigmoid(proj(g0, 2 * d_model))


def _inproj(x3, g, w_bf, q_dtype, cache_layout):
    b, tp, d = x3.shape
    d_in = w_bf.shape[1]
    tm = _pick(tp, SEQ_TILES)
    row = lambda n: pl.BlockSpec((1, tm, n), lambda i, j: (i, j, 0))
    outs = [(RWKV_COLS, F32), (D_QK, q_dtype), (D_QK, F32), (D_VA, F32), (D_QK, BF16), (D_VA, BF16), (2 * d, F32)]
    specs = [row(n) for n, _ in outs]
    shapes = [jax.ShapeDtypeStruct((b, tp, n), dt) for n, dt in outs]
    if cache_layout:
        specs[2] = pl.BlockSpec((1, D_QK, tm), lambda i, j: (i, 0, j))
        shapes[2] = jax.ShapeDtypeStruct((b, D_QK, tp), F32)
        specs[3] = pl.BlockSpec((1, tm * H_ATT, DV_ATT), lambda i, j: (i, j, 0))
        shapes[3] = jax.ShapeDtypeStruct((b, tp * H_ATT, DV_ATT), F32)
    return pl.pallas_call(
        functools.partial(_inproj_kernel, d_model=d, cache_layout=cache_layout),
        grid=(b, tp // tm),
        in_specs=[row(d), _const_spec((1, d)), _const_spec((d, d_in))],
        out_specs=specs,
        out_shape=shapes,
        compiler_params=_cparams(("parallel", "parallel")),
        name="inproj",
    )(x3, g.reshape(1, d), w_bf)


def _prep_kernel(p_ref, prev_ref, mu_ref, w0_ref, ww2_ref, a0_ref, wa2_ref, kk_ref, ka_ref, rk_ref, sel_ref,
                 r_out, w_out, k_out, v_out, kk_out, kka_out, bonus_out, *, shift_rows):
    p = p_ref[0]
    if shift_rows:
        first = pl.program_id(1) == 0
        before = jnp.where(first, 0.0, prev_ref[0][7:8])
        rolled = pltpu.roll(p, 1, axis=0)
        rows = lax.broadcasted_iota(jnp.int32, p.shape, 0)
        prev = jnp.where(rows == 0, before, rolled)
    else:
        prev = prev_ref[0]
    m = p + (prev - p) * mu_ref[...]
    r = m[:, :D_RWKV]
    k = m[:, D_RWKV:2 * D_RWKV]
    v = m[:, 2 * D_RWKV:3 * D_RWKV]
    w_lo = m[:, 3 * D_RWKV:3 * D_RWKV + LORA]
    a_lo = m[:, 3 * D_RWKV + LORA:]
    sel = sel_ref[...]
    w = jnp.exp(-DECAY_SCALE * jax.nn.sigmoid(
        w0_ref[...] + jnp.dot(jnp.tanh(w_lo), ww2_ref[...], precision=HI, preferred_element_type=F32)))
    a = jax.nn.sigmoid(a0_ref[...] + jnp.dot(a_lo, wa2_ref[...], precision=HI, preferred_element_type=F32))
    kk = k * kk_ref[...]
    k2 = k * (1.0 + (a - 1.0) * ka_ref[...])
    ss = _seg_sum2(kk * kk, sel)
    kkn = kk / jnp.maximum(jnp.sqrt(ss), 1e-12)
    r_out[0] = r
    w_out[0] = w
    k_out[0] = k2
    v_out[0] = v
    kk_out[0] = kkn
    kka_out[0] = kkn * a
    bonus_out[0] = _seg_sum2(r * k2 * rk_ref[...], sel) * v


def _prep(p3, prev3, lw, sel512):
    b, tp, c = p3.shape
    shift_rows = prev3 is None
    tt = _pick(tp, SEQ_TILES)
    cur = pl.BlockSpec((1, tt, c), lambda i, j: (i, j, 0))
    if shift_rows:
        nb8 = tt // 8
        prev_spec = pl.BlockSpec((1, 8, c), lambda i, j: (i, jnp.maximum(j * nb8 - 1, 0), 0))
        prev_arr = p3
    else:
        prev_spec = cur
        prev_arr = prev3
    out = pl.BlockSpec((1, tt, D_RWKV), lambda i, j: (i, j, 0))
    vec = lambda n: _const_spec((1, n))
    return pl.pallas_call(
        functools.partial(_prep_kernel, shift_rows=shift_rows),
        grid=(b, tp // tt),
        in_specs=[cur, prev_spec, vec(c), vec(D_RWKV), _const_spec((LORA, D_RWKV)), vec(D_RWKV),
                  _const_spec((LORA, D_RWKV)), vec(D_RWKV), vec(D_RWKV), vec(D_RWKV),
                  _const_spec((D_RWKV, D_RWKV))],
        out_specs=[out] * 7,
        out_shape=[jax.ShapeDtypeStruct((b, tp, D_RWKV), F32)] * 7,
        compiler_params=_cparams(("parallel", "parallel")),
        name="prep",
    )(p3, prev_arr, lw['shift_mu'], lw['w0'], lw['w_w2'], lw['a0'], lw['w_a2'], lw['k_k'], lw['k_a'],
      lw['r_k'], sel512)


WKV_W = 4 * N_RWKV


def _wkv_kernel(r_ref, w_ref, k_ref, v_ref, kk_ref, kka_ref, s0_ref, y_ref, sT_ref, s_scr, ybuf, vb_scr, *, t_valid,
                nb, tc, gsz):
    c = pl.program_id(1)
    nq = D_RWKV // WKV_W

    @pl.when(c == 0)
    def _():
        s_scr[...] = s0_ref[...].reshape(s_scr.shape)

    nsteps = jnp.clip(t_valid - c * tc, 0, tc)

    @pl.when(nsteps < tc)
    def _():
        y_ref[...] = jnp.zeros_like(y_ref)

    sel = _seg_ones(WKV_W, N_RWKV).astype(BF16)
    eye = jnp.where(lax.broadcasted_iota(jnp.int32, (N_RWKV, WKV_W), 0)
                    == lax.broadcasted_iota(jnp.int32, (N_RWKV, WKV_W), 1) % N_RWKV, 1.0, 0.0).astype(F32)[None]
    eye_b = eye.astype(BF16)
    lane8 = lax.broadcasted_iota(jnp.int32, (8, WKV_W), 1)
    lane_blk = (lane8 % N_RWKV) // 8
    eye8 = jnp.where(lax.broadcasted_iota(jnp.int32, (8, WKV_W), 0) == lane8 % 8, 1.0, 0.0).astype(F32)
    chains = [(b, q) for b in range(nb) for q in range(nq)]

    nch = len(chains)
    tile = (nch, N_RWKV, WKV_W)

    def seg_sums(*xs):
        lhs = jnp.concatenate([x.astype(BF16).reshape(-1, WKV_W) for x in xs], axis=0)
        o = jnp.dot(lhs, sel, preferred_element_type=F32)
        n = nch * N_RWKV
        return [o[j * n:(j + 1) * n].reshape(tile) for j in range(len(xs))]

    def group(g, carry):
        t0 = pl.multiple_of(g * gsz, gsz)

        def rows(ref, i):
            return jnp.stack([ref[b, pl.ds(t0, gsz), WKV_W * q:WKV_W * (q + 1)][i:i + 1] for b, q in chains])

        vbs = seg_sums(*[rows(v_ref, i).astype(BF16) * eye_b for i in range(gsz)])
        for i in range(gsz):
            vb_scr[i] = vbs[i]
        s = s_scr[...]
        sk, = seg_sums(s * rows(kk_ref, 0))
        for i in range(gsz):
            s = s * rows(w_ref, i) - sk * rows(kka_ref, i) + vb_scr[i] * rows(k_ref, i)
            if i + 1 < gsz:
                sk, yb = seg_sums(s * rows(kk_ref, i + 1), s * rows(r_ref, i))
            else:
                yb, = seg_sums(s * rows(r_ref, i))
            yb4 = yb.reshape(nch, N_RWKV // 8, 8, WKV_W)
            diag = yb4[:, 0]
            for blk in range(1, N_RWKV // 8):
                diag = jnp.where(lane_blk == blk, yb4[:, blk], diag)
            ybuf[:, i:i + 1, :] = jnp.sum(diag * eye8, axis=1, keepdims=True)
        s_scr[...] = s
        for n, (b, q) in enumerate(chains):
            y_ref[b, pl.ds(t0, gsz), WKV_W * q:WKV_W * (q + 1)] = ybuf[n]
        return carry

    lax.fori_loop(0, nsteps // gsz, group, 0)

    @pl.when(c == pl.num_programs(1) - 1)
    def _():
        sT_ref[...] = s_scr[...].reshape(sT_ref.shape)


WKV_G = 8
WKV_UNROLL = 2


def _wkv_group_kernel(r_ref, w_ref, k_ref, v_ref, kk_ref, kka_ref, s0_ref, y_ref, sT_ref, s_scr, *, t_valid, nb, tc):
    c = pl.program_id(1)
    nq = D_RWKV // WKV_W
    g_sz = WKV_G
    chains = [(b, q) for b in range(nb) for q in range(nq)]
    nch = len(chains)
    heads = WKV_W // N_RWKV

    @pl.when(c == 0)
    def _():
        s_scr[...] = s0_ref[...].reshape(s_scr.shape)

    nsteps = jnp.clip(t_valid - c * tc, 0, tc)

    @pl.when(nsteps < tc)
    def _():
        y_ref[...] = jnp.zeros_like(y_ref)

    sel = _seg_ones(WKV_W, N_RWKV).astype(BF16)
    row8 = lax.broadcasted_iota(jnp.int32, (g_sz, WKV_W), 0)
    head16 = lax.broadcasted_iota(jnp.int32, (2 * g_sz, WKV_W), 1) // N_RWKV
    head64 = lax.broadcasted_iota(jnp.int32, (N_RWKV, WKV_W), 1) // N_RWKV

    def by_step(tiles, i):
        return jnp.concatenate([t[i:i + 1] for t in tiles], axis=0)

    pairs = [(kind, t, s) for t in range(g_sz) for s in range(t + 1) for kind in 'abcd' if s < t or kind in 'cd']
    pair_index = {p: i for i, p in enumerate(pairs)}

    def prelude(t0):
        tile = lambda ref, b, q: ref[b, pl.ds(t0, g_sz), WKV_W * q:WKV_W * (q + 1)]

        kap_p, rh_p, kh_p, bh_p, v_p, gam_p = [], [], [], [], [], []
        for b, q in chains:
            w = tile(w_ref, b, q)
            gam, shift = w, 1
            while shift < g_sz:
                gam = gam * jnp.where(row8 >= shift, pltpu.roll(gam, shift, axis=0), 1.0)
                shift *= 2
            gprev = jnp.where(row8 >= 1, pltpu.roll(gam, 1, axis=0), 1.0)
            ginv = 1.0 / gam
            kap_p.append(tile(kk_ref, b, q) * gprev)
            rh_p.append(tile(r_ref, b, q) * gam)
            kh_p.append(tile(k_ref, b, q) * ginv)
            bh_p.append(tile(kka_ref, b, q) * ginv)
            v_p.append(tile(v_ref, b, q))
            gam_p.append(gam)

        kap_s = [by_step(kap_p, i) for i in range(g_sz)]
        rh_s = [by_step(rh_p, i) for i in range(g_sz)]
        kh_s = [by_step(kh_p, i) for i in range(g_sz)]
        bh_s = [by_step(bh_p, i) for i in range(g_sz)]
        v_s = [by_step(v_p, i) for i in range(g_sz)]

        prods = [(kap_s[t] if kind in 'ab' else rh_s[t]) * (kh_s[s] if kind in 'ac' else bh_s[s])
                 for kind, t, s in pairs]
        dots = jnp.dot(jnp.concatenate(prods, axis=0).astype(BF16), sel, preferred_element_type=F32)
        coef = lambda kind, t, s: dots[pair_index[kind, t, s] * nch:(pair_index[kind, t, s] + 1) * nch]
        return kap_p, rh_p, kh_p, bh_p, v_p, gam_p, v_s, coef

    def chain(t0, pre):
        kap_p, rh_p, kh_p, bh_p, v_p, gam_p, v_s, coef = pre

        u0_p, y0_p = [], []
        for n in range(nch):
            sb = s_scr[n].astype(BF16)
            lhs = jnp.concatenate([kap_p[n], rh_p[n]], axis=0)
            lhs4 = jnp.concatenate([jnp.where(head16 == h, lhs, 0.0) for h in range(heads)], axis=0).astype(BF16)
            out = lax.dot_general(lhs4, jnp.concatenate([sb] * heads, axis=0), (((1,), (1,)), ((), ())),
                                  preferred_element_type=F32)
            uy = out[:2 * g_sz]
            for h in range(1, heads):
                uy = jnp.where(head16 == h, out[2 * g_sz * h:2 * g_sz * (h + 1)], uy)
            u0_p.append(uy[:g_sz])
            y0_p.append(uy[g_sz:])

        us, ys = [], []
        for t in range(g_sz):
            u = by_step(u0_p, t)
            y = by_step(y0_p, t)
            for s in range(t):
                u = u + coef('a', t, s) * v_s[s] - coef('b', t, s) * us[s]
            us.append(u)
            for s in range(t + 1):
                y = y + coef('c', t, s) * v_s[s] - coef('d', t, s) * us[s]
            ys.append(y)

        for n, (b, q) in enumerate(chains):
            u_n = by_step(us, n)
            x = jnp.concatenate([v_p[n], -u_n], axis=0).astype(BF16)
            z = jnp.concatenate([kh_p[n], bh_p[n]], axis=0).astype(BF16)
            full = lax.dot_general(x, z, (((0,), (0,)), ((), ())), preferred_element_type=F32)
            ds = full[:N_RWKV]
            for h in range(1, heads):
                ds = jnp.where(head64 == h, full[N_RWKV * h:N_RWKV * (h + 1)], ds)
            s_scr[n] = (s_scr[n] + ds) * gam_p[n][g_sz - 1:g_sz]
            y_ref[b, pl.ds(t0, g_sz), WKV_W * q:WKV_W * (q + 1)] = by_step(ys, n)

    def trip(i, carry):
        for sub in range(WKV_UNROLL):
            t0 = pl.multiple_of(i * (WKV_UNROLL * g_sz) + sub * g_sz, g_sz)
            chain(t0, prelude(t0))
        return carry

    lax.fori_loop(0, nsteps // (WKV_UNROLL * g_sz), trip, 0)

    @pl.when(c == pl.num_programs(1) - 1)
    def _():
        sT_ref[...] = s_scr[...].reshape(sT_ref.shape)


def _wkv(r, w, k, v, kk, kka, s0, t_valid):
    b, tp, _ = r.shape
    nq = D_RWKV // WKV_W
    nb = _pick(b, (4, 2, 1))
    tc = _pick(tp, (128,))
    grouped = tc % WKV_G == 0
    gsz = WKV_G if grouped else tc
    trip = WKV_G * WKV_UNROLL if grouped else tc
    assert t_valid % trip == 0 and tc % trip == 0, "sequence length must fill whole loop trips"
    seq = pl.BlockSpec((nb, tc, D_RWKV), lambda i, c: (i, c, 0))
    st = pl.BlockSpec((nb, nq, N_RWKV, WKV_W), lambda i, c: (i, 0, 0, 0))
    state = pltpu.VMEM((nb * nq, N_RWKV, WKV_W), F32)
    if grouped:
        body = functools.partial(_wkv_group_kernel, t_valid=t_valid, nb=nb, tc=tc)
        scratch = [state]
    else:
        body = functools.partial(_wkv_kernel, t_valid=t_valid, nb=nb, tc=tc, gsz=gsz)
        scratch = [state, pltpu.VMEM((nb * nq, gsz, WKV_W), F32), pltpu.VMEM((gsz, nb * nq, N_RWKV, WKV_W), F32)]
    return pl.pallas_call(
        body,
        grid=(b // nb, tp // tc),
        in_specs=[seq] * 6 + [st],
        out_specs=[seq, st],
        out_shape=[jax.ShapeDtypeStruct((b, tp, D_RWKV), F32),
                   jax.ShapeDtypeStruct((b, nq, N_RWKV, WKV_W), F32)],
        scratch_shapes=scratch,
        compiler_params=_cparams(("parallel", "arbitrary")),
        name="wkv",
    )(r, w, k, v, kk, kka, s0)


def _to_groups(s):
    b = s.shape[0]
    nq = D_RWKV // WKV_W
    return s.reshape(b, nq, H_RWKV // nq, N_RWKV, N_RWKV).transpose(0, 1, 3, 2, 4).reshape(b, nq, N_RWKV, WKV_W)


def _from_groups(s):
    b = s.shape[0]
    nq = D_RWKV // WKV_W
    return s.reshape(b, nq, N_RWKV, H_RWKV // nq, N_RWKV).transpose(0, 1, 3, 2, 4).reshape(
        b, H_RWKV, N_RWKV, N_RWKV)


def _lam(lq1, lk1, lq2, lk2, lam_init):
    return (jnp.exp(jnp.sum(lq1 * lk1, axis=-1, keepdims=True))
            - jnp.exp(jnp.sum(lq2 * lk2, axis=-1, keepdims=True)) + lam_init)


def _subln(o, g, lam_init):
    return o * lax.rsqrt(jnp.mean(o * o, axis=-1, keepdims=True) + RMS_EPS) * g * (1.0 - lam_init)


def _flash_kernel(q_ref, k_ref, v_ref, lq1, lk1, lq2, lk2, g_ref, o_ref, q2_scr, m_scr, l_scr, acc_scr, *, lam_init,
                  tq):
    qi = pl.program_id(1)
    m_scr[...] = jnp.full_like(m_scr, NEG)
    l_scr[...] = jnp.zeros_like(l_scr)
    acc_scr[...] = jnp.zeros_like(acc_scr)
    lane = lax.broadcasted_iota(jnp.int32, (tq, LANES), 1)
    for h in range(H_ATT):
        qh = q_ref[0, :, LANES * h:LANES * (h + 1)]
        q2_scr[h, :tq] = jnp.where(lane < DH_ATT, qh, jnp.zeros_like(qh))
        q2_scr[h, tq:] = jnp.where(lane >= DH_ATT, qh, jnp.zeros_like(qh))

    def block(k0, tk, masked):
        if masked:
            tri = lax.broadcasted_iota(jnp.int32, (tq, tk), 1) <= lax.broadcasted_iota(jnp.int32, (tq, tk), 0)
            keep = jnp.concatenate([tri, tri], axis=0)
        ones = jnp.ones((tk, LANES), BF16)
        for h in range(H_ATT):
            ln = slice(LANES * h, LANES * (h + 1))
            kh = k_ref[0, pl.ds(k0, tk), ln]
            v1 = jnp.concatenate([v_ref[0, pl.ds(k0, tk), ln], ones], axis=1)
            s = lax.dot_general(q2_scr[h], kh, (((1,), (1,)), ((), ())), preferred_element_type=F32)
            if masked:
                s = jnp.where(keep, s, NEG)
            m_prev = m_scr[h]
            m_new = jnp.maximum(m_prev, jnp.max(s, axis=-1, keepdims=True))
            alpha = jnp.exp(m_prev - m_new)
            p = jnp.exp((s - jnp.concatenate([m_new] * (tk // LANES), axis=1)).astype(BF16))
            pv = jnp.dot(p, v1, preferred_element_type=F32)
            acc_scr[h] = alpha * acc_scr[h] + pv[:, :LANES]
            l_scr[h] = alpha * l_scr[h] + pv[:, LANES:]
            m_scr[h] = m_new

    def body(kj, carry):
        block(pl.multiple_of(kj * (2 * tq), tq), 2 * tq, False)
        return carry

    lax.fori_loop(0, qi // 2, body, 0)

    @pl.when(qi % 2 == 1)
    def _():
        block(pl.multiple_of((qi - 1) * tq, tq), tq, False)

    block(pl.multiple_of(qi * tq, tq), tq, True)
    lam = _lam(lq1[...], lk1[...], lq2[...], lk2[...], lam_init)
    for h in range(H_ATT):
        o = acc_scr[h] / l_scr[h]
        o_ref[0, :, LANES * h:LANES * (h + 1)] = _subln(o[:tq] - lam * o[tq:], g_ref[...], lam_init)


def _flash(qb, kb, vb, lam_rows, g, lam_init):
    b, tp, _ = qb.shape
    tq = _pick(tp, SEQ_TILES)
    qspec = pl.BlockSpec((1, tq, D_QK), lambda i, q: (i, q, 0))
    kspec = pl.BlockSpec((1, tp, D_QK), lambda i, q: (i, 0, 0))
    vec = _const_spec((1, DH_ATT))
    return pl.pallas_call(
        functools.partial(_flash_kernel, lam_init=lam_init, tq=tq),
        grid=(b, tp // tq),
        in_specs=[qspec, kspec, kspec, vec, vec, vec, vec, _const_spec((1, DV_ATT))],
        out_specs=qspec,
        out_shape=jax.ShapeDtypeStruct((b, tp, D_VA), F32),
        scratch_shapes=[pltpu.VMEM((H_ATT, 2 * tq, LANES), BF16)] + [pltpu.VMEM((H_ATT, 2 * tq, LANES), F32)] * 3,
        compiler_params=_cparams(("parallel", "arbitrary")),
        name="flash",
    )(qb, kb, vb, *lam_rows, g)


PAGED_ROWS = 16


def _paged_kernel(pt_ref, q_ref, kn_ref, vn_ref, *rest, n_pages, lam_init):
    k_refs, v_refs = rest[:n_pages], rest[n_pages:2 * n_pages]
    lq1, lk1, lq2, lk2, g_ref, o_ref = rest[2 * n_pages:]
    page = v_refs[0].shape[0] // H_ATT
    rows = lax.broadcasted_iota(jnp.int32, (PAGED_ROWS, D_QK), 0)
    cols = lax.broadcasted_iota(jnp.int32, (PAGED_ROWS, D_QK), 1) // DH_ATT
    q8 = jnp.where(rows == cols, q_ref[0], 0.0)
    q_hi = q8.astype(BF16)
    q2 = jnp.concatenate([q_hi, (q8 - q_hi.astype(F32)).astype(BF16)], axis=0)
    scores = []
    for i in range(n_pages):
        kt = k_refs[i][...].reshape(D_QK, page).astype(BF16)
        s2 = jnp.dot(q2, kt, preferred_element_type=F32)
        scores.append(s2[:PAGED_ROWS] + s2[PAGED_ROWS:])
    s_self = jnp.sum(q8 * kn_ref[0], axis=-1, keepdims=True)
    m = s_self
    for s in scores:
        m = jnp.maximum(m, jnp.max(s, axis=-1, keepdims=True))
    p_self = jnp.exp(s_self - m)
    l = p_self
    acc = [p_self * vn_ref[0][:, LANES * h:LANES * (h + 1)] for h in range(H_ATT)]
    for i in range(n_pages):
        p = jnp.exp(scores[i] - m)
        l = l + jnp.sum(p, axis=-1, keepdims=True)
        pb = p.astype(BF16)
        for h in range(H_ATT):
            vh = v_refs[i][pl.ds(h, page, stride=H_ATT), :].astype(BF16)
            acc[h] = acc[h] + jnp.dot(pb, vh, preferred_element_type=F32)
    lam = _lam(lq1[...], lk1[...], lq2[...], lk2[...], lam_init)
    for h in range(H_ATT):
        o = acc[h] / l
        d = o[2 * h:2 * h + 1] - lam * o[2 * h + 1:2 * h + 2]
        o_ref[0, :, LANES * h:LANES * (h + 1)] = _subln(d, g_ref[...], lam_init)


def _paged(q, kn, vn, cache_kt, cache_v2, layer, page_table, lam_rows, g, lam_init):
    db = q.shape[0]
    n_pages = page_table.shape[1]
    page = cache_kt.shape[-1]
    tok = pl.BlockSpec((1, 1, D_QK), lambda b, pt: (b, 0, 0))
    kpg = lambda i: pl.BlockSpec((None, None, 2 * H_ATT, DH_ATT, page),
                                 lambda b, pt: (layer, pt[b * n_pages + i], 0, 0, 0))
    vpg = lambda i: pl.BlockSpec((None, None, page * H_ATT, DV_ATT), lambda b, pt: (layer, pt[b * n_pages + i], 0, 0))
    vec = pl.BlockSpec((1, DH_ATT), lambda b, pt: (0, 0))
    grid_spec = pltpu.PrefetchScalarGridSpec(
        num_scalar_prefetch=1,
        grid=(db,),
        in_specs=([tok, tok, tok] + [kpg(i) for i in range(n_pages)] + [vpg(i) for i in range(n_pages)]
                  + [vec, vec, vec, vec, pl.BlockSpec((1, DV_ATT), lambda b, pt: (0, 0))]),
        out_specs=tok,
    )
    return pl.pallas_call(
        functools.partial(_paged_kernel, n_pages=n_pages, lam_init=lam_init),
        grid_spec=grid_spec,
        out_shape=jax.ShapeDtypeStruct((db, 1, D_VA), F32),
        compiler_params=_cparams(("parallel",)),
        name="paged",
    )(page_table.reshape(-1), q, kn, vn, *([cache_kt] * n_pages), *([cache_v2] * n_pages), *lam_rows, g)


def _mix_kernel(x_ref, y_ref, bonus_ref, ob_ref, gate_ref, lng_ref, lnb_ref, wpa_ref, wpb_ref, wo_ref,
                nm_ref, wup_ref, wdn_ref, nf_ref, xo_ref, *, d_model, ff_chunk, final):
    y = y_ref[...]
    low = lax.broadcasted_iota(jnp.int32, (y.shape[0], LANES), 1) < N_RWKV

    def head_mean(z):
        blocks = []
        for j in range(D_RWKV // LANES):
            zb = z[:, LANES * j:LANES * (j + 1)]
            lo = jnp.sum(jnp.where(low, zb, 0.0), axis=-1, keepdims=True)
            hi = jnp.sum(jnp.where(low, 0.0, zb), axis=-1, keepdims=True)
            blocks.append(jnp.where(low, lo, hi))
        return jnp.concatenate(blocks, axis=1) * (1.0 / N_RWKV)

    d = y - head_mean(y)
    var = head_mean(d * d)
    o_a = d * lax.rsqrt(var + LNX_EPS) * lng_ref[...] + lnb_ref[...] + bonus_ref[...]
    gates = gate_ref[...]
    merged = (gates[:, :d_model] * jnp.dot(o_a.astype(BF16), wpa_ref[...], preferred_element_type=F32)
              + gates[:, d_model:] * jnp.dot(ob_ref[...].astype(BF16), wpb_ref[...], preferred_element_type=F32))
    x = x_ref[...] + jnp.dot(merged.astype(BF16), wo_ref[...], preferred_element_type=F32)
    h2 = (x * lax.rsqrt(jnp.mean(x * x, axis=-1, keepdims=True) + RMS_EPS) * nm_ref[...]).astype(BF16)
    d_ff = wup_ref.shape[1]
    for c in range(d_ff // ff_chunk):
        cs = slice(c * ff_chunk, (c + 1) * ff_chunk)
        u = jnp.maximum(jnp.dot(h2, wup_ref[:, cs], preferred_element_type=F32), 0.0)
        x = x + jnp.dot((u * u).astype(BF16), wdn_ref[cs, :], preferred_element_type=F32)
    if final:
        x = x * lax.rsqrt(jnp.mean(x * x, axis=-1, keepdims=True) + RMS_EPS) * nf_ref[...]
    xo_ref[...] = x


def _mix(x2, y2, bonus2, ob2, gates2, lw, norm_final, final):
    m, d = x2.shape
    d_ff = lw['w_up_b'].shape[1]
    tm = _pick(m, (512, 256, 128))
    row = lambda n: pl.BlockSpec((tm, n), lambda i: (i, 0))
    cs = _const_spec
    return pl.pallas_call(
        functools.partial(_mix_kernel, d_model=d, ff_chunk=min(d_ff, 1024), final=final),
        grid=(m // tm,),
        in_specs=[row(d), row(D_RWKV), row(D_RWKV), row(D_VA), row(2 * d), cs((1, D_RWKV)), cs((1, D_RWKV)),
                  cs((D_RWKV, d)), cs((D_VA, d)), cs((d, d)), cs((1, d)),
                  cs((d, d_ff)), cs((d_ff, d)), cs((1, d))],
        out_specs=row(d),
        out_shape=jax.ShapeDtypeStruct((m, d), F32),
        compiler_params=_cparams(("parallel",)),
        name="mix",
    )(x2, y2, bonus2, ob2, gates2, lw['lnx_g'], lw['lnx_b'], lw['w_pa_b'], lw['w_pb_b'], lw['w_o_b'],
      lw['norm_mlp'], lw['w_up_b'], lw['w_down_b'], norm_final)


def _layer(x3, lw, sel512, lam_init, t_valid, prev3, s0_pairs, attend, norm_final, final):
    b, tp, d = x3.shape
    m = b * tp
    prompt = prev3 is None
    pr3, q, k, v, kb, vb, gates = _inproj(x3, lw['norm_mix'], lw['w_in_b'], BF16 if prompt else F32, prompt)
    r, w, k2, vv, kk, kka, bonus = _prep(pr3, prev3, lw, sel512)
    if prev3 is not None:
        seqs = lambda a: a.reshape(m, 1, D_RWKV)
        y, s_fin = _wkv(seqs(r), seqs(w), seqs(k2), seqs(vv), seqs(kk), seqs(kka), s0_pairs, 1)
    else:
        y, s_fin = _wkv(r, w, k2, vv, kk, kka, s0_pairs, t_valid)
    ob = attend(q, k, v, kb, vb)
    xo = _mix(x3.reshape(m, d), y.reshape(m, D_RWKV), bonus.reshape(m, D_RWKV), ob.reshape(m, D_VA),
              gates.reshape(m, 2 * d), lw, norm_final, final)
    return xo.reshape(b, tp, d), k, v, s_fin, pr3


def kernel(x_prompt, x_sample, cache_k, cache_v, state_wkv, state_shift, page_table, meta_tokens, norm_mix, w_in,
           shift_mu, w0, w_w2, a0, w_a2, k_k, k_a, r_k, lnx_g, lnx_b, lambda_q1, lambda_k1, lambda_q2, lambda_k2,
           subln_g, w_pa, w_pb, w_o, norm_mlp, w_up, w_down, norm_final):
    bsz, seq, d = x_prompt.shape
    db = x_sample.shape[0]
    depth = w_in.shape[0]
    n_meta = meta_tokens.shape[0]
    t = seq + n_meta
    tp = -(-t // LANES) * LANES
    n_pool, page = cache_k.shape[1], cache_k.shape[2]

    xp = jnp.pad(x_prompt, ((0, 0), (n_meta, tp - t), (0, 0)))
    xp = lax.dynamic_update_slice(
        xp, jnp.broadcast_to(meta_tokens[None].astype(x_prompt.dtype), (bsz, n_meta, d)), (0, 0, 0))
    xs = x_sample.reshape(1, db, d)
    cache_kt = cache_k.transpose(0, 1, 3, 4, 2)
    cache_v2 = cache_v.reshape(depth, n_pool, page * H_ATT, DV_ATT)
    sel512 = _seg_ones(D_RWKV, N_RWKV).astype(BF16)
    nf = norm_final.reshape(1, d)
    zero_state = jnp.zeros((bsz, D_RWKV // WKV_W, N_RWKV, WKV_W), F32)

    kp_l, vp_l, wp_l, sp_l, ks_l, vs_l, ws_l, ss_l = ([] for _ in range(8))
    for l in range(depth):
        row = lambda a: a[l].reshape(1, -1)
        lw = {
            'norm_mix': norm_mix[l], 'w_in_b': w_in[l].astype(BF16), 'shift_mu': row(shift_mu), 'w0': row(w0),
            'w_w2': w_w2[l], 'a0': row(a0), 'w_a2': w_a2[l], 'k_k': row(k_k), 'k_a': row(k_a), 'r_k': row(r_k),
            'lnx_g': row(lnx_g), 'lnx_b': row(lnx_b), 'w_pa_b': w_pa[l].astype(BF16),
            'w_pb_b': w_pb[l].astype(BF16), 'w_o_b': w_o[l].astype(BF16), 'norm_mlp': row(norm_mlp),
            'w_up_b': w_up[l].astype(BF16), 'w_down_b': w_down[l].astype(BF16),
        }
        lam_init = 0.8 - 0.6 * math.exp(-0.3 * l)
        lam_rows = (row(lambda_q1), row(lambda_k1), row(lambda_q2), row(lambda_k2))
        g = row(subln_g)
        final = l == depth - 1

        attend_p = lambda q, k, v, kb, vb: _flash(q, kb, vb, lam_rows, g, lam_init)
        xp, k_new, v_new, s_fin, pr3 = _layer(xp, lw, sel512, lam_init, t, None, zero_state, attend_p, nf, final)
        kp_l.append(k_new)
        vp_l.append(v_new)
        wp_l.append(_from_groups(s_fin))
        sp_l.append(pr3[:, t - 1])

        attend_s = lambda q, k, v, kb, vb: _paged(
            q.reshape(db, 1, D_QK), k.reshape(db, 1, D_QK), v.reshape(db, 1, D_VA), cache_kt, cache_v2, l,
            page_table, lam_rows, g, lam_init).reshape(1, db, D_VA)
        xs, k_new, v_new, s_fin, pr3 = _layer(xs, lw, sel512, lam_init, 1, state_shift[l][None],
                                              _to_groups(state_wkv[l]), attend_s, nf, final)
        ks_l.append(k_new.reshape(db, 1, 2 * H_ATT, DH_ATT))
        vs_l.append(v_new.reshape(db, 1, H_ATT, DV_ATT))
        ws_l.append(_from_groups(s_fin).astype(state_wkv.dtype))
        ss_l.append(pr3[0])

    y_prompt = xp[:, n_meta:t]
    y_sample = xs.reshape(db, 1, d)
    k_prompt = jnp.stack(kp_l).reshape(depth, bsz, 2 * H_ATT, DH_ATT, tp)[..., :t].transpose(0, 1, 4, 2, 3)
    v_prompt = jnp.stack(vp_l).reshape(depth, bsz, tp, H_ATT, DV_ATT)[:, :, :t]
    return (y_prompt, y_sample, k_prompt, v_prompt, jnp.stack(wp_l), jnp.stack(sp_l),
            jnp.stack(ks_l), jnp.stack(vs_l), jnp.stack(ws_l), jnp.stack(ss_l))
```

```python
import functools
import math

import jax
import jax.numpy as jnp
from jax import lax
from jax.experimental import pallas as pl
from jax.experimental.pallas import tpu as pltpu

F32 = jnp.float32
BF16 = jnp.bfloat16
HI = lax.Precision.HIGHEST

H_RWKV = 8
N_RWKV = 64
D_RWKV = H_RWKV * N_RWKV
LORA = 64
RWKV_COLS = 3 * D_RWKV + 2 * LORA
DECAY_SCALE = 0.606531
LNX_EPS = 64e-5
H_ATT = 4
DH_ATT = 64
DV_ATT = 2 * DH_ATT
D_QK = 2 * H_ATT * DH_ATT
D_VA = H_ATT * DV_ATT
ATT_SCALE = DH_ATT ** -0.5
NEG = -1e30
RMS_EPS = 1e-6

LANES = 128
SEQ_TILES = (384, 256, 128)
VMEM_LIMIT = 56 * 1024 * 1024


def _cparams(sem):
    return pltpu.CompilerParams(dimension_semantics=sem, vmem_limit_bytes=VMEM_LIMIT)


def _pick(n, cands):
    for c in cands:
        if n % c == 0:
            return c
    return n


def _const_spec(shape):
    nd = len(shape)
    return pl.BlockSpec(shape, lambda *_: (0,) * nd, pipeline_mode=pl.Buffered(1))


def _seg_ones(n, seg, scale=1.0):
    i = lax.broadcasted_iota(jnp.int32, (n, n), 0) // seg
    j = lax.broadcasted_iota(jnp.int32, (n, n), 1) // seg
    return jnp.where(i == j, scale, 0.0).astype(F32)


def _seg_sum2(x, sel_b):
    hi = x.astype(BF16)
    lo = (x - hi.astype(F32)).astype(BF16)
    o = jnp.dot(jnp.concatenate([hi, lo], axis=0), sel_b, preferred_element_type=F32)
    return o[:x.shape[0]] + o[x.shape[0]:]


def _inproj_kernel(x_ref, g_ref, w_ref, pr_ref, q_ref, k_ref, v_ref, kb_ref, vb_ref, gate_ref, *, d_model,
                   cache_layout):
    x = x_ref[0]
    hn = x * lax.rsqrt(jnp.mean(x * x, axis=-1, keepdims=True) + RMS_EPS) * g_ref[...]
    hb = hn.astype(BF16)
    tm = x.shape[0]

    def proj(lo, n):
        return jnp.dot(hb, w_ref[:, lo:lo + n], preferred_element_type=F32)

    q0 = RWKV_COLS
    k0 = q0 + D_QK
    v0 = k0 + D_QK
    g0 = v0 + D_VA
    pr_ref[0] = proj(0, RWKV_COLS)
    q_ref[0] = (proj(q0, D_QK) * ATT_SCALE).astype(q_ref.dtype)
    k = proj(k0, D_QK)
    kb_ref[0] = k.astype(BF16)
    v = proj(v0, D_VA)
    vb_ref[0] = v.astype(BF16)
    if cache_layout:
        k_ref[0] = k.T
        for h in range(H_ATT):
            v_ref[0, pl.ds(h, tm, stride=H_ATT), :] = v[:, DV_ATT * h:DV_ATT * (h + 1)]
    else:
        k_ref[0] = k
        v_ref[0] = v
    gate_ref[0] = jax.nn.sigmoid(proj(g0, 2 * d_model))


def _inproj(x3, g, w_bf, q_dtype, cache_layout):
    b, tp, d = x3.shape
    d_in = w_bf.shape[1]
    tm = _pick(tp, SEQ_TILES)
    row = lambda n: pl.BlockSpec((1, tm, n), lambda i, j: (i, j, 0))
    outs = [(RWKV_COLS, F32), (D_QK, q_dtype), (D_QK, F32), (D_VA, F32), (D_QK, BF16), (D_VA, BF16), (2 * d, F32)]
    specs = [row(n) for n, _ in outs]
    shapes = [jax.ShapeDtypeStruct((b, tp, n), dt) for n, dt in outs]
    if cache_layout:
        specs[2] = pl.BlockSpec((1, D_QK, tm), lambda i, j: (i, 0, j))
        shapes[2] = jax.ShapeDtypeStruct((b, D_QK, tp), F32)
        specs[3] = pl.BlockSpec((1, tm * H_ATT, DV_ATT), lambda i, j: (i, j, 0))
        shapes[3] = jax.ShapeDtypeStruct((b, tp * H_ATT, DV_ATT), F32)
    return pl.pallas_call(
        functools.partial(_inproj_kernel, d_model=d, cache_layout=cache_layout),
        grid=(b, tp // tm),
        in_specs=[row(d), _const_spec((1, d)), _const_spec((d, d_in))],
        out_specs=specs,
        out_shape=shapes,
        compiler_params=_cparams(("parallel", "parallel")),
        name="inproj",
    )(x3, g.reshape(1, d), w_bf)


def _prep_kernel(p_ref, prev_ref, mu_ref, w0_ref, ww2_ref, a0_ref, wa2_ref, kk_ref, ka_ref, rk_ref, sel_ref,
                 r_out, w_out, k_out, v_out, kk_out, kka_out, bonus_out, *, shift_rows):
    p = p_ref[0]
    if shift_rows:
        first = pl.program_id(1) == 0
        before = jnp.where(first, 0.0, prev_ref[0][7:8])
        rolled = pltpu.roll(p, 1, axis=0)
        rows = lax.broadcasted_iota(jnp.int32, p.shape, 0)
        prev = jnp.where(rows == 0, before, rolled)
    else:
        prev = prev_ref[0]
    m = p + (prev - p) * mu_ref[...]
    r = m[:, :D_RWKV]
    k = m[:, D_RWKV:2 * D_RWKV]
    v = m[:, 2 * D_RWKV:3 * D_RWKV]
    w_lo = m[:, 3 * D_RWKV:3 * D_RWKV + LORA]
    a_lo = m[:, 3 * D_RWKV + LORA:]
    low = lax.broadcasted_iota(jnp.int32, (p.shape[0], LANES), 1) < N_RWKV

    def head_sum(z):
        blocks = []
        for j in range(D_RWKV // LANES):
            zb = z[:, LANES * j:LANES * (j + 1)]
            lo = jnp.sum(jnp.where(low, zb, 0.0), axis=-1, keepdims=True)
            hi = jnp.sum(jnp.where(low, 0.0, zb), axis=-1, keepdims=True)
            blocks.append(jnp.where(low, lo, hi))
        return jnp.concatenate(blocks, axis=1)

    w = jnp.exp(-DECAY_SCALE * jax.nn.sigmoid(
        w0_ref[...] + jnp.dot(jnp.tanh(w_lo), ww2_ref[...], precision=HI, preferred_element_type=F32)))
    a = jax.nn.sigmoid(a0_ref[...] + jnp.dot(a_lo, wa2_ref[...], precision=HI, preferred_element_type=F32))
    kk = k * kk_ref[...]
    k2 = k * (1.0 + (a - 1.0) * ka_ref[...])
    ss = head_sum(kk * kk)
    kkn = kk / jnp.maximum(jnp.sqrt(ss), 1e-12)
    r_out[0] = r
    w_out[0] = w
    k_out[0] = k2
    v_out[0] = v
    kk_out[0] = kkn
    kka_out[0] = kkn * a
    bonus_out[0] = head_sum(r * k2 * rk_ref[...]) * v


def _prep(p3, prev3, lw, sel512):
    b, tp, c = p3.shape
    shift_rows = prev3 is None
    tt = _pick(tp, SEQ_TILES)
    cur = pl.BlockSpec((1, tt, c), lambda i, j: (i, j, 0))
    if shift_rows:
        nb8 = tt // 8
        prev_spec = pl.BlockSpec((1, 8, c), lambda i, j: (i, jnp.maximum(j * nb8 - 1, 0), 0))
        prev_arr = p3
    else:
        prev_spec = cur
        prev_arr = prev3
    out = pl.BlockSpec((1, tt, D_RWKV), lambda i, j: (i, j, 0))
    vec = lambda n: _const_spec((1, n))
    return pl.pallas_call(
        functools.partial(_prep_kernel, shift_rows=shift_rows),
        grid=(b, tp // tt),
        in_specs=[cur, prev_spec, vec(c), vec(D_RWKV), _const_spec((LORA, D_RWKV)), vec(D_RWKV),
                  _const_spec((LORA, D_RWKV)), vec(D_RWKV), vec(D_RWKV), vec(D_RWKV),
                  _const_spec((D_RWKV, D_RWKV))],
        out_specs=[out] * 7,
        out_shape=[jax.ShapeDtypeStruct((b, tp, D_RWKV), F32)] * 7,
        compiler_params=_cparams(("parallel", "parallel")),
        name="prep",
    )(p3, prev_arr, lw['shift_mu'], lw['w0'], lw['w_w2'], lw['a0'], lw['w_a2'], lw['k_k'], lw['k_a'],
      lw['r_k'], sel512)


WKV_W = 4 * N_RWKV


def _wkv_kernel(r_ref, w_ref, k_ref, v_ref, kk_ref, kka_ref, s0_ref, y_ref, sT_ref, s_scr, ybuf, vb_scr, *, t_valid,
                nb, tc, gsz):
    c = pl.program_id(1)
    nq = D_RWKV // WKV_W

    @pl.when(c == 0)
    def _():
        s_scr[...] = s0_ref[...].reshape(s_scr.shape)

    nsteps = jnp.clip(t_valid - c * tc, 0, tc)

    @pl.when(nsteps < tc)
    def _():
        y_ref[...] = jnp.zeros_like(y_ref)

    sel = _seg_ones(WKV_W, N_RWKV).astype(BF16)
    eye = jnp.where(lax.broadcasted_iota(jnp.int32, (N_RWKV, WKV_W), 0)
                    == lax.broadcasted_iota(jnp.int32, (N_RWKV, WKV_W), 1) % N_RWKV, 1.0, 0.0).astype(F32)[None]
    eye_b = eye.astype(BF16)
    lane8 = lax.broadcasted_iota(jnp.int32, (8, WKV_W), 1)
    lane_blk = (lane8 % N_RWKV) // 8
    eye8 = jnp.where(lax.broadcasted_iota(jnp.int32, (8, WKV_W), 0) == lane8 % 8, 1.0, 0.0).astype(F32)
    chains = [(b, q) for b in range(nb) for q in range(nq)]

    nch = len(chains)
    tile = (nch, N_RWKV, WKV_W)

    def seg_sums(*xs):
        lhs = jnp.concatenate([x.astype(BF16).reshape(-1, WKV_W) for x in xs], axis=0)
        o = jnp.dot(lhs, sel, preferred_element_type=F32)
        n = nch * N_RWKV
        return [o[j * n:(j + 1) * n].reshape(tile) for j in range(len(xs))]

    def group(g, carry):
        t0 = pl.multiple_of(g * gsz, gsz)

        def rows(ref, i):
            return jnp.stack([ref[b, pl.ds(t0, gsz), WKV_W * q:WKV_W * (q + 1)][i:i + 1] for b, q in chains])

        vbs = seg_sums(*[rows(v_ref, i).astype(BF16) * eye_b for i in range(gsz)])
        for i in range(gsz):
            vb_scr[i] = vbs[i]
        s = s_scr[...]
        sk, = seg_sums(s * rows(kk_ref, 0))
        for i in range(gsz):
            s = s * rows(w_ref, i) - sk * rows(kka_ref, i) + vb_scr[i] * rows(k_ref, i)
            if i + 1 < gsz:
                sk, yb = seg_sums(s * rows(kk_ref, i + 1), s * rows(r_ref, i))
            else:
                yb, = seg_sums(s * rows(r_ref, i))
            yb4 = yb.reshape(nch, N_RWKV // 8, 8, WKV_W)
            diag = yb4[:, 0]
            for blk in range(1, N_RWKV // 8):
                diag = jnp.where(lane_blk == blk, yb4[:, blk], diag)
            ybuf[:, i:i + 1, :] = jnp.sum(diag * eye8, axis=1, keepdims=True)
        s_scr[...] = s
        for n, (b, q) in enumerate(chains):
            y_ref[b, pl.ds(t0, gsz), WKV_W * q:WKV_W * (q + 1)] = ybuf[n]
        return carry

    lax.fori_loop(0, nsteps // gsz, group, 0)

    @pl.when(c == pl.num_programs(1) - 1)
    def _():
        sT_ref[...] = s_scr[...].reshape(sT_ref.shape)


WKV_G = 8
WKV_UNROLL = 2


def _wkv_group_kernel(r_ref, w_ref, k_ref, v_ref, kk_ref, kka_ref, s0_ref, y_ref, sT_ref, s_scr, *, t_valid, nb, tc):
    c = pl.program_id(1)
    nq = D_RWKV // WKV_W
    g_sz = WKV_G
    chains = [(b, q) for b in range(nb) for q in range(nq)]
    nch = len(chains)
    heads = WKV_W // N_RWKV

    @pl.when(c == 0)
    def _():
        s_scr[...] = s0_ref[...].reshape(s_scr.shape)

    nsteps = jnp.clip(t_valid - c * tc, 0, tc)

    @pl.when(nsteps < tc)
    def _():
        y_ref[...] = jnp.zeros_like(y_ref)

    sel = _seg_ones(WKV_W, N_RWKV).astype(BF16)
    row8 = lax.broadcasted_iota(jnp.int32, (g_sz, WKV_W), 0)
    head16 = lax.broadcasted_iota(jnp.int32, (2 * g_sz, WKV_W), 1) // N_RWKV
    head64 = lax.broadcasted_iota(jnp.int32, (N_RWKV, WKV_W), 1) // N_RWKV

    def by_step(tiles, i):
        return jnp.concatenate([t[i:i + 1] for t in tiles], axis=0)

    pairs = [(kind, t, s) for t in range(g_sz) for s in range(t + 1) for kind in 'abcd' if s < t or kind in 'cd']
    pair_index = {p: i for i, p in enumerate(pairs)}

    def prelude(t0):
        tile = lambda ref, b, q: ref[b, pl.ds(t0, g_sz), WKV_W * q:WKV_W * (q + 1)]

        kap_p, rh_p, kh_p, bh_p, v_p, gam_p = [], [], [], [], [], []
        for b, q in chains:
            w = tile(w_ref, b, q)
            gam, shift = w, 1
            while shift < g_sz:
                gam = gam * jnp.where(row8 >= shift, pltpu.roll(gam, shift, axis=0), 1.0)
                shift *= 2
            gprev = jnp.where(row8 >= 1, pltpu.roll(gam, 1, axis=0), 1.0)
            ginv = 1.0 / gam
            kap_p.append(tile(kk_ref, b, q) * gprev)
            rh_p.append(tile(r_ref, b, q) * gam)
            kh_p.append(tile(k_ref, b, q) * ginv)
            bh_p.append(tile(kka_ref, b, q) * ginv)
            v_p.append(tile(v_ref, b, q))
            gam_p.append(gam)

        kap_s = [by_step(kap_p, i) for i in range(g_sz)]
        rh_s = [by_step(rh_p, i) for i in range(g_sz)]
        kh_s = [by_step(kh_p, i) for i in range(g_sz)]
        bh_s = [by_step(bh_p, i) for i in range(g_sz)]
        v_s = [by_step(v_p, i) for i in range(g_sz)]

        prods = [(kap_s[t] if kind in 'ab' else rh_s[t]) * (kh_s[s] if kind in 'ac' else bh_s[s])
                 for kind, t, s in pairs]
        dots = jnp.dot(jnp.concatenate(prods, axis=0).astype(BF16), sel, preferred_element_type=F32)
        coef = lambda kind, t, s: dots[pair_index[kind, t, s] * nch:(pair_index[kind, t, s] + 1) * nch]
        return kap_p, rh_p, kh_p, bh_p, v_p, gam_p, v_s, coef

    def chain(t0, pre):
        kap_p, rh_p, kh_p, bh_p, v_p, gam_p, v_s, coef = pre

        u0_p, y0_p = [], []
        for n in range(nch):
            sb = s_scr[n].astype(BF16)
            lhs = jnp.concatenate([kap_p[n], rh_p[n]], axis=0)
            lhs4 = jnp.concatenate([jnp.where(head16 == h, lhs, 0.0) for h in range(heads)], axis=0).astype(BF16)
            out = lax.dot_general(lhs4, jnp.concatenate([sb] * heads, axis=0), (((1,), (1,)), ((), ())),
                                  preferred_element_type=F32)
            uy = out[:2 * g_sz]
            for h in range(1, heads):
                uy = jnp.where(head16 == h, out[2 * g_sz * h:2 * g_sz * (h + 1)], uy)
            u0_p.append(uy[:g_sz])
            y0_p.append(uy[g_sz:])

        us, ys = [], []
        for t in range(g_sz):
            u = by_step(u0_p, t)
            y = by_step(y0_p, t)
            for s in range(t):
                u = u + coef('a', t, s) * v_s[s] - coef('b', t, s) * us[s]
            us.append(u)
            for s in range(t + 1):
                y = y + coef('c', t, s) * v_s[s] - coef('d', t, s) * us[s]
            ys.append(y)

        for n, (b, q) in enumerate(chains):
            u_n = by_step(us, n)
            x = jnp.concatenate([v_p[n], -u_n], axis=0).astype(BF16)
            z = jnp.concatenate([kh_p[n], bh_p[n]], axis=0).astype(BF16)
            full = lax.dot_general(x, z, (((0,), (0,)), ((), ())), preferred_element_type=F32)
            ds = full[:N_RWKV]
            for h in range(1, heads):
                ds = jnp.where(head64 == h, full[N_RWKV * h:N_RWKV * (h + 1)], ds)
            s_scr[n] = (s_scr[n] + ds) * gam_p[n][g_sz - 1:g_sz]
            y_ref[b, pl.ds(t0, g_sz), WKV_W * q:WKV_W * (q + 1)] = by_step(ys, n)

    def trip(i, carry):
        for sub in range(WKV_UNROLL):
            t0 = pl.multiple_of(i * (WKV_UNROLL * g_sz) + sub * g_sz, g_sz)
            chain(t0, prelude(t0))
        return carry

    lax.fori_loop(0, nsteps // (WKV_UNROLL * g_sz), trip, 0)

    @pl.when(c == pl.num_programs(1) - 1)
    def _():
        sT_ref[...] = s_scr[...].reshape(sT_ref.shape)


def _wkv(r, w, k, v, kk, kka, s0, t_valid):
    b, tp, _ = r.shape
    nq = D_RWKV // WKV_W
    nb = _pick(b, (4, 2, 1))
    tc = _pick(tp, (128,))
    grouped = tc % WKV_G == 0
    gsz = WKV_G if grouped else tc
    trip = WKV_G * WKV_UNROLL if grouped else tc
    assert t_valid % trip == 0 and tc % trip == 0, "sequence length must fill whole loop trips"
    seq = pl.BlockSpec((nb, tc, D_RWKV), lambda i, c: (i, c, 0))
    st = pl.BlockSpec((nb, nq, N_RWKV, WKV_W), lambda i, c: (i, 0, 0, 0))
    state = pltpu.VMEM((nb * nq, N_RWKV, WKV_W), F32)
    if grouped:
        body = functools.partial(_wkv_group_kernel, t_valid=t_valid, nb=nb, tc=tc)
        scratch = [state]
    else:
        body = functools.partial(_wkv_kernel, t_valid=t_valid, nb=nb, tc=tc, gsz=gsz)
        scratch = [state, pltpu.VMEM((nb * nq, gsz, WKV_W), F32), pltpu.VMEM((gsz, nb * nq, N_RWKV, WKV_W), F32)]
    return pl.pallas_call(
        body,
        grid=(b // nb, tp // tc),
        in_specs=[seq] * 6 + [st],
        out_specs=[seq, st],
        out_shape=[jax.ShapeDtypeStruct((b, tp, D_RWKV), F32),
                   jax.ShapeDtypeStruct((b, nq, N_RWKV, WKV_W), F32)],
        scratch_shapes=scratch,
        compiler_params=_cparams(("parallel", "arbitrary")),
        name="wkv",
    )(r, w, k, v, kk, kka, s0)


def _to_groups(s):
    b = s.shape[0]
    nq = D_RWKV // WKV_W
    return s.reshape(b, nq, H_RWKV // nq, N_RWKV, N_RWKV).transpose(0, 1, 3, 2, 4).reshape(b, nq, N_RWKV, WKV_W)


def _from_groups(s):
    b = s.shape[0]
    nq = D_RWKV // WKV_W
    return s.reshape(b, nq, N_RWKV, H_RWKV // nq, N_RWKV).transpose(0, 1, 3, 2, 4).reshape(
        b, H_RWKV, N_RWKV, N_RWKV)


def _lam(lq1, lk1, lq2, lk2, lam_init):
    return (jnp.exp(jnp.sum(lq1 * lk1, axis=-1, keepdims=True))
            - jnp.exp(jnp.sum(lq2 * lk2, axis=-1, keepdims=True)) + lam_init)


def _subln(o, g, lam_init):
    return o * lax.rsqrt(jnp.mean(o * o, axis=-1, keepdims=True) + RMS_EPS) * g * (1.0 - lam_init)


def _flash_kernel(q_ref, k_ref, v_ref, lq1, lk1, lq2, lk2, g_ref, o_ref, q2_scr, m_scr, l_scr, acc_scr, *, lam_init,
                  tq):
    qi = pl.program_id(1)
    m_scr[...] = jnp.full_like(m_scr, NEG)
    l_scr[...] = jnp.zeros_like(l_scr)
    acc_scr[...] = jnp.zeros_like(acc_scr)
    lane = lax.broadcasted_iota(jnp.int32, (tq, LANES), 1)
    for h in range(H_ATT):
        qh = q_ref[0, :, LANES * h:LANES * (h + 1)]
        q2_scr[h, :tq] = jnp.where(lane < DH_ATT, qh, jnp.zeros_like(qh))
        q2_scr[h, tq:] = jnp.where(lane >= DH_ATT, qh, jnp.zeros_like(qh))

    def block(k0, tk, masked):
        if masked:
            tri = lax.broadcasted_iota(jnp.int32, (tq, tk), 1) <= lax.broadcasted_iota(jnp.int32, (tq, tk), 0)
            keep = jnp.concatenate([tri, tri], axis=0)
        ones = jnp.ones((tk, LANES), BF16)
        for h in range(H_ATT):
            ln = slice(LANES * h, LANES * (h + 1))
            kh = k_ref[0, pl.ds(k0, tk), ln]
            v1 = jnp.concatenate([v_ref[0, pl.ds(k0, tk), ln], ones], axis=1)
            s = lax.dot_general(q2_scr[h], kh, (((1,), (1,)), ((), ())), preferred_element_type=F32)
            if masked:
                s = jnp.where(keep, s, NEG)
            m_prev = m_scr[h]
            m_new = jnp.maximum(m_prev, jnp.max(s, axis=-1, keepdims=True))
            alpha = jnp.exp(m_prev - m_new)
            p = jnp.exp((s - jnp.concatenate([m_new] * (tk // LANES), axis=1)).astype(BF16))
            pv = jnp.dot(p, v1, preferred_element_type=F32)
            acc_scr[h] = alpha * acc_scr[h] + pv[:, :LANES]
            l_scr[h] = alpha * l_scr[h] + pv[:, LANES:]
            m_scr[h] = m_new

    def body(kj, carry):
        block(pl.multiple_of(kj * (2 * tq), tq), 2 * tq, False)
        return carry

    lax.fori_loop(0, qi // 2, body, 0)

    @pl.when(qi % 2 == 1)
    def _():
        block(pl.multiple_of((qi - 1) * tq, tq), tq, False)

    block(pl.multiple_of(qi * tq, tq), tq, True)
    lam = _lam(lq1[...], lk1[...], lq2[...], lk2[...], lam_init)
    for h in range(H_ATT):
        o = acc_scr[h] / l_scr[h]
        o_ref[0, :, LANES * h:LANES * (h + 1)] = _subln(o[:tq] - lam * o[tq:], g_ref[...], lam_init)


def _flash(qb, kb, vb, lam_rows, g, lam_init):
    b, tp, _ = qb.shape
    tq = _pick(tp, SEQ_TILES)
    qspec = pl.BlockSpec((1, tq, D_QK), lambda i, q: (i, q, 0))
    kspec = pl.BlockSpec((1, tp, D_QK), lambda i, q: (i, 0, 0))
    vec = _const_spec((1, DH_ATT))
    return pl.pallas_call(
        functools.partial(_flash_kernel, lam_init=lam_init, tq=tq),
        grid=(b, tp // tq),
        in_specs=[qspec, kspec, kspec, vec, vec, vec, vec, _const_spec((1, DV_ATT))],
        out_specs=qspec,
        out_shape=jax.ShapeDtypeStruct((b, tp, D_VA), F32),
        scratch_shapes=[pltpu.VMEM((H_ATT, 2 * tq, LANES), BF16)] + [pltpu.VMEM((H_ATT, 2 * tq, LANES), F32)] * 3,
        compiler_params=_cparams(("parallel", "arbitrary")),
        name="flash",
    )(qb, kb, vb, *lam_rows, g)


PAGED_ROWS = 16


def _paged_kernel(pt_ref, q_ref, kn_ref, vn_ref, *rest, n_pages, lam_init):
    k_refs, v_refs = rest[:n_pages], rest[n_pages:2 * n_pages]
    lq1, lk1, lq2, lk2, g_ref, o_ref = rest[2 * n_pages:]
    page = v_refs[0].shape[0] // H_ATT
    rows = lax.broadcasted_iota(jnp.int32, (PAGED_ROWS, D_QK), 0)
    cols = lax.broadcasted_iota(jnp.int32, (PAGED_ROWS, D_QK), 1) // DH_ATT
    q8 = jnp.where(rows == cols, q_ref[0], 0.0)
    q_hi = q8.astype(BF16)
    q2 = jnp.concatenate([q_hi, (q8 - q_hi.astype(F32)).astype(BF16)], axis=0)
    scores = []
    for i in range(n_pages):
        kt = k_refs[i][...].reshape(D_QK, page).astype(BF16)
        s2 = jnp.dot(q2, kt, preferred_element_type=F32)
        scores.append(s2[:PAGED_ROWS] + s2[PAGED_ROWS:])
    s_self = jnp.sum(q8 * kn_ref[0], axis=-1, keepdims=True)
    m = s_self
    for s in scores:
        m = jnp.maximum(m, jnp.max(s, axis=-1, keepdims=True))
    p_self = jnp.exp(s_self - m)
    l = p_self
    acc = [p_self * vn_ref[0][:, LANES * h:LANES * (h + 1)] for h in range(H_ATT)]
    for i in range(n_pages):
        p = jnp.exp(scores[i] - m)
        l = l + jnp.sum(p, axis=-1, keepdims=True)
        pb = p.astype(BF16)
        for h in range(H_ATT):
            vh = v_refs[i][pl.ds(h, page, stride=H_ATT), :].astype(BF16)
            acc[h] = acc[h] + jnp.dot(pb, vh, preferred_element_type=F32)
    lam = _lam(lq1[...], lk1[...], lq2[...], lk2[...], lam_init)
    for h in range(H_ATT):
        o = acc[h] / l
        d = o[2 * h:2 * h + 1] - lam * o[2 * h + 1:2 * h + 2]
        o_ref[0, :, LANES * h:LANES * (h + 1)] = _subln(d, g_ref[...], lam_init)


def _paged(q, kn, vn, cache_kt, cache_v2, layer, page_table, lam_rows, g, lam_init):
    db = q.shape[0]
    n_pages = page_table.shape[1]
    page = cache_kt.shape[-1]
    tok = pl.BlockSpec((1, 1, D_QK), lambda b, pt: (b, 0, 0))
    kpg = lambda i: pl.BlockSpec((None, None, 2 * H_ATT, DH_ATT, page),
                                 lambda b, pt: (layer, pt[b * n_pages + i], 0, 0, 0))
    vpg = lambda i: pl.BlockSpec((None, None, page * H_ATT, DV_ATT), lambda b, pt: (layer, pt[b * n_pages + i], 0, 0))
    vec = pl.BlockSpec((1, DH_ATT), lambda b, pt: (0, 0))
    grid_spec = pltpu.PrefetchScalarGridSpec(
        num_scalar_prefetch=1,
        grid=(db,),
        in_specs=([tok, tok, tok] + [kpg(i) for i in range(n_pages)] + [vpg(i) for i in range(n_pages)]
                  + [vec, vec, vec, vec, pl.BlockSpec((1, DV_ATT), lambda b, pt: (0, 0))]),
        out_specs=tok,
    )
    return pl.pallas_call(
        functools.partial(_paged_kernel, n_pages=n_pages, lam_init=lam_init),
        grid_spec=grid_spec,
        out_shape=jax.ShapeDtypeStruct((db, 1, D_VA), F32),
        compiler_params=_cparams(("parallel",)),
        name="paged",
    )(page_table.reshape(-1), q, kn, vn, *([cache_kt] * n_pages), *([cache_v2] * n_pages), *lam_rows, g)


def _mix_kernel(x_ref, y_ref, bonus_ref, ob_ref, gate_ref, lng_ref, lnb_ref, wpa_ref, wpb_ref, wo_ref,
                nm_ref, wup_ref, wdn_ref, nf_ref, xo_ref, *, d_model, ff_chunk, final):
    y = y_ref[...]
    low = lax.broadcasted_iota(jnp.int32, (y.shape[0], LANES), 1) < N_RWKV

    def head_mean(z):
        blocks = []
        for j in range(D_RWKV // LANES):
            zb = z[:, LANES * j:LANES * (j + 1)]
            lo = jnp.sum(jnp.where(low, zb, 0.0), axis=-1, keepdims=True)
            hi = jnp.sum(jnp.where(low, 0.0, zb), axis=-1, keepdims=True)
            blocks.append(jnp.where(low, lo, hi))
        return jnp.concatenate(blocks, axis=1) * (1.0 / N_RWKV)

    d = y - head_mean(y)
    var = head_mean(d * d)
    o_a = d * lax.rsqrt(var + LNX_EPS) * lng_ref[...] + lnb_ref[...] + bonus_ref[...]
    gates = gate_ref[...]
    merged = (gates[:, :d_model] * jnp.dot(o_a.astype(BF16), wpa_ref[...], preferred_element_type=F32)
              + gates[:, d_model:] * jnp.dot(ob_ref[...].astype(BF16), wpb_ref[...], preferred_element_type=F32))
    x = x_ref[...] + jnp.dot(merged.astype(BF16), wo_ref[...], preferred_element_type=F32)
    h2 = (x * lax.rsqrt(jnp.mean(x * x, axis=-1, keepdims=True) + RMS_EPS) * nm_ref[...]).astype(BF16)
    d_ff = wup_ref.shape[1]
    for c in range(d_ff // ff_chunk):
        cs = slice(c * ff_chunk, (c + 1) * ff_chunk)
        u = jnp.maximum(jnp.dot(h2, wup_ref[:, cs], preferred_element_type=F32), 0.0)
        x = x + jnp.dot((u * u).astype(BF16), wdn_ref[cs, :], preferred_element_type=F32)
    if final:
        x = x * lax.rsqrt(jnp.mean(x * x, axis=-1, keepdims=True) + RMS_EPS) * nf_ref[...]
    xo_ref[...] = x


def _mix(x2, y2, bonus2, ob2, gates2, lw, norm_final, final):
    m, d = x2.shape
    d_ff = lw['w_up_b'].shape[1]
    tm = _pick(m, (512, 256, 128))
    row = lambda n: pl.BlockSpec((tm, n), lambda i: (i, 0))
    cs = _const_spec
    return pl.pallas_call(
        functools.partial(_mix_kernel, d_model=d, ff_chunk=min(d_ff, 1024), final=final),
        grid=(m // tm,),
        in_specs=[row(d), row(D_RWKV), row(D_RWKV), row(D_VA), row(2 * d), cs((1, D_RWKV)), cs((1, D_RWKV)),
                  cs((D_RWKV, d)), cs((D_VA, d)), cs((d, d)), cs((1, d)),
                  cs((d, d_ff)), cs((d_ff, d)), cs((1, d))],
        out_specs=row(d),
        out_shape=jax.ShapeDtypeStruct((m, d), F32),
        compiler_params=_cparams(("parallel",)),
        name="mix",
    )(x2, y2, bonus2, ob2, gates2, lw['lnx_g'], lw['lnx_b'], lw['w_pa_b'], lw['w_pb_b'], lw['w_o_b'],
      lw['norm_mlp'], lw['w_up_b'], lw['w_down_b'], norm_final)


def _layer(x3, lw, sel512, lam_init, t_valid, prev3, s0_pairs, attend, norm_final, final):
    b, tp, d = x3.shape
    m = b * tp
    prompt = prev3 is None
    pr3, q, k, v, kb, vb, gates = _inproj(x3, lw['norm_mix'], lw['w_in_b'], BF16 if prompt else F32, prompt)
    r, w, k2, vv, kk, kka, bonus = _prep(pr3, prev3, lw, sel512)
    if prev3 is not None:
        seqs = lambda a: a.reshape(m, 1, D_RWKV)
        y, s_fin = _wkv(seqs(r), seqs(w), seqs(k2), seqs(vv), seqs(kk), seqs(kka), s0_pairs, 1)
    else:
        y, s_fin = _wkv(r, w, k2, vv, kk, kka, s0_pairs, t_valid)
    ob = attend(q, k, v, kb, vb)
    xo = _mix(x3.reshape(m, d), y.reshape(m, D_RWKV), bonus.reshape(m, D_RWKV), ob.reshape(m, D_VA),
              gates.reshape(m, 2 * d), lw, norm_final, final)
    return xo.reshape(b, tp, d), k, v, s_fin, pr3


def kernel(x_prompt, x_sample, cache_k, cache_v, state_wkv, state_shift, page_table, meta_tokens, norm_mix, w_in,
           shift_mu, w0, w_w2, a0, w_a2, k_k, k_a, r_k, lnx_g, lnx_b, lambda_q1, lambda_k1, lambda_q2, lambda_k2,
           subln_g, w_pa, w_pb, w_o, norm_mlp, w_up, w_down, norm_final):
    bsz, seq, d = x_prompt.shape
    db = x_sample.shape[0]
    depth = w_in.shape[0]
    n_meta = meta_tokens.shape[0]
    t = seq + n_meta
    tp = -(-t // LANES) * LANES
    n_pool, page = cache_k.shape[1], cache_k.shape[2]

    xp = jnp.pad(x_prompt, ((0, 0), (n_meta, tp - t), (0, 0)))
    xp = lax.dynamic_update_slice(
        xp, jnp.broadcast_to(meta_tokens[None].astype(x_prompt.dtype), (bsz, n_meta, d)), (0, 0, 0))
    xs = x_sample.reshape(1, db, d)
    cache_kt = cache_k.transpose(0, 1, 3, 4, 2)
    cache_v2 = cache_v.reshape(depth, n_pool, page * H_ATT, DV_ATT)
    sel512 = _seg_ones(D_RWKV, N_RWKV).astype(BF16)
    nf = norm_final.reshape(1, d)
    zero_state = jnp.zeros((bsz, D_RWKV // WKV_W, N_RWKV, WKV_W), F32)

    kp_l, vp_l, wp_l, sp_l, ks_l, vs_l, ws_l, ss_l = ([] for _ in range(8))
    for l in range(depth):
        row = lambda a: a[l].reshape(1, -1)
        lw = {
            'norm_mix': norm_mix[l], 'w_in_b': w_in[l].astype(BF16), 'shift_mu': row(shift_mu), 'w0': row(w0),
            'w_w2': w_w2[l], 'a0': row(a0), 'w_a2': w_a2[l], 'k_k': row(k_k), 'k_a': row(k_a), 'r_k': row(r_k),
            'lnx_g': row(lnx_g), 'lnx_b': row(lnx_b), 'w_pa_b': w_pa[l].astype(BF16),
            'w_pb_b': w_pb[l].astype(BF16), 'w_o_b': w_o[l].astype(BF16), 'norm_mlp': row(norm_mlp),
            'w_up_b': w_up[l].astype(BF16), 'w_down_b': w_down[l].astype(BF16),
        }
        lam_init = 0.8 - 0.6 * math.exp(-0.3 * l)
        lam_rows = (row(lambda_q1), row(lambda_k1), row(lambda_q2), row(lambda_k2))
        g = row(subln_g)
        final = l == depth - 1

        attend_p = lambda q, k, v, kb, vb: _flash(q, kb, vb, lam_rows, g, lam_init)
        xp, k_new, v_new, s_fin, pr3 = _layer(xp, lw, sel512, lam_init, t, None, zero_state, attend_p, nf, final)
        kp_l.append(k_new)
        vp_l.append(v_new)
        wp_l.append(_from_groups(s_fin))
        sp_l.append(pr3[:, t - 1])

        attend_s = lambda q, k, v, kb, vb: _paged(
            q.reshape(db, 1, D_QK), k.reshape(db, 1, D_QK), v.reshape(db, 1, D_VA), cache_kt, cache_v2, l,
            page_table, lam_rows, g, lam_init).reshape(1, db, D_VA)
        xs, k_new, v_new, s_fin, pr3 = _layer(xs, lw, sel512, lam_init, 1, state_shift[l][None],
                                              _to_groups(state_wkv[l]), attend_s, nf, final)
        ks_l.append(k_new.reshape(db, 1, 2 * H_ATT, DH_ATT))
        vs_l.append(v_new.reshape(db, 1, H_ATT, DV_ATT))
        ws_l.append(_from_groups(s_fin).astype(state_wkv.dtype))
        ss_l.append(pr3[0])

    y_prompt = xp[:, n_meta:t]
    y_sample = xs.reshape(db, 1, d)
    k_prompt = jnp.stack(kp_l).reshape(depth, bsz, 2 * H_ATT, DH_ATT, tp)[..., :t].transpose(0, 1, 4, 2, 3)
    v_prompt = jnp.stack(vp_l).reshape(depth, bsz, tp, H_ATT, DV_ATT)[:, :, :t]
    return (y_prompt, y_sample, k_prompt, v_prompt, jnp.stack(wp_l), jnp.stack(sp_l),
            jnp.stack(ks_l), jnp.stack(vs_l), jnp.stack(ws_l), jnp.stack(ss_l))
```
